```python
import jax
import jax.numpy as jnp
from jax import lax
import numpy as np

D_MODEL = 1024
BATCH = 8
SEQ = 2048
DEPTH = 4

N_MEM = 256
EPS = 1e-6
NEG_INF = -1e30

HEAD_DIM = 64
ROPE_DIM = HEAD_DIM // 4
ROPE_THETA = 500000.0

A_HEADS = 8
A_WIDTH = A_HEADS * HEAD_DIM
MOBA_BLOCK = 256
MOBA_TOPK = 3
MOBA_QCHUNK = 32

B_GROUPS = 8
B_WIDTH = B_GROUPS * HEAD_DIM
GMLP_CHUNK = 128

GLA_HEADS = 4
GLA_DK = (D_MODEL // 2) // GLA_HEADS
GLA_DV = D_MODEL // GLA_HEADS
GLA_LOWRANK = 16
GLA_TAU = 16.0
GLA_CHUNK = 64

X_HEADS = 4
X_HEAD_DIM = D_MODEL // X_HEADS

D_FF = -(-8 * D_MODEL // (3 * 256)) * 256

EV_IN = 3 * A_WIDTH + 2 * B_WIDTH
OD_IN = 2 * GLA_HEADS * GLA_DK + 2 * GLA_HEADS * GLA_DV

kernel_name = 'hybrid_moba_gmlp_gla_trunk'


def rmsnorm(x, g):
    xf = x.astype(jnp.float32)
    y = xf * lax.rsqrt(jnp.mean(xf * xf, axis=-1, keepdims=True) + EPS)
    return (y * g.astype(jnp.float32)).astype(x.dtype)


def layernorm(x, g, b):
    xf = x.astype(jnp.float32)
    mu = jnp.mean(xf, axis=-1, keepdims=True)
    var = jnp.mean(jnp.square(xf - mu), axis=-1, keepdims=True)
    y = (xf - mu) * lax.rsqrt(var + EPS) * g.astype(jnp.float32) + b.astype(jnp.float32)
    return y.astype(x.dtype)


def partial_rope(x, positions):
    half = ROPE_DIM // 2
    inv_freq = ROPE_THETA ** (-jnp.arange(half, dtype=jnp.float32) * 2.0 / ROPE_DIM)
    ang = positions.astype(jnp.float32)[:, None, :, None] * inv_freq
    cos, sin = jnp.cos(ang), jnp.sin(ang)
    xf = x.astype(jnp.float32)
    x1, x2 = xf[..., :half], xf[..., half:ROPE_DIM]
    out = jnp.concatenate([x1 * cos - x2 * sin, x2 * cos + x1 * sin, xf[..., ROPE_DIM:]], axis=-1)
    return out.astype(x.dtype)


def moba_attention(q, k, v):
    Bn, H, S, Dh = q.shape
    S_pad = -(-S // MOBA_BLOCK) * MOBA_BLOCK
    pad = [(0, 0), (0, 0), (0, S_pad - S), (0, 0)]
    q, k, v = jnp.pad(q, pad), jnp.pad(k, pad), jnp.pad(v, pad)
    nb = S_pad // MOBA_BLOCK
    topk = min(MOBA_TOPK, nb)
    scale = Dh ** -0.5
    kb = k.reshape(Bn, H, nb, MOBA_BLOCK, Dh)
    vb = v.reshape(Bn, H, nb, MOBA_BLOCK, Dh)
    k_mean = jnp.mean(kb.astype(jnp.float32), axis=3)
    gate = jnp.einsum('bhsd,bhnd->bhsn', q.astype(jnp.float32), k_mean)
    own = jnp.arange(S_pad) // MOBA_BLOCK
    past = jnp.arange(nb)[None, :] < own[:, None]
    gate = jnp.where(past[None, None], gate, NEG_INF)
    _, sel = lax.top_k(gate, topk)
    valid = sel < own[None, None, :, None]

    nq = S_pad // MOBA_QCHUNK
    q_c = q.reshape(Bn, H, nq, MOBA_QCHUNK, Dh).transpose(2, 0, 1, 3, 4)
    sel_c = sel.reshape(Bn, H, nq, MOBA_QCHUNK, topk).transpose(2, 0, 1, 3, 4)
    valid_c = valid.reshape(Bn, H, nq, MOBA_QCHUNK, topk).transpose(2, 0, 1, 3, 4)
    chunk_ids = jnp.arange(nq, dtype=jnp.int32)
    bi = jnp.arange(Bn)[:, None, None, None]
    hi = jnp.arange(H)[None, :, None, None]
    key_idx = jnp.arange(MOBA_BLOCK)

    def one_chunk(args):
        qc, selc, validc, ci = args
        blk = (ci * MOBA_QCHUNK) // MOBA_BLOCK
        k_own = lax.dynamic_index_in_dim(kb, blk, axis=2, keepdims=False)
        v_own = lax.dynamic_index_in_dim(vb, blk, axis=2, keepdims=False)
        qpos = (ci * MOBA_QCHUNK) % MOBA_BLOCK + jnp.arange(MOBA_QCHUNK)
        causal = key_idx[None, :] <= qpos[:, None]
        s_own = jnp.einsum('bhqd,bhkd->bhqk', qc, k_own, preferred_element_type=jnp.float32) * scale
        s_own = jnp.where(causal[None, None], s_own, NEG_INF)
        k_sel = kb[bi, hi, selc]
        v_sel = vb[bi, hi, selc]
        s_sel = jnp.einsum('bhqd,bhqnkd->bhqnk', qc, k_sel, preferred_element_type=jnp.float32) * scale
        s_sel = jnp.where(validc[..., None], s_sel, NEG_INF)
        logits = jnp.concatenate([s_own, s_sel.reshape(Bn, H, MOBA_QCHUNK, topk * MOBA_BLOCK)], axis=-1)
        p = jax.nn.softmax(logits, axis=-1)
        p_own = p[..., :MOBA_BLOCK].astype(v.dtype)
        p_sel = p[..., MOBA_BLOCK:].reshape(Bn, H, MOBA_QCHUNK, topk, MOBA_BLOCK).astype(v.dtype)
        out = (jnp.einsum('bhqk,bhkd->bhqd', p_own, v_own, preferred_element_type=jnp.float32)
               + jnp.einsum('bhqnk,bhqnkd->bhqd', p_sel, v_sel, preferred_element_type=jnp.float32))
        return out.astype(q.dtype)

    out = lax.map(one_chunk, (q_c, sel_c, valid_c, chunk_ids))
    out = out.transpose(1, 2, 0, 3, 4).reshape(Bn, H, S_pad, Dh)
    return out[:, :, :S]


def chunked_spatial_gating(u, v, w_s, b_s, ln_g, ln_b):
    Bn, S, G, Dg = v.shape
    nc = S // GMLP_CHUNK
    vn = layernorm(v, ln_g, ln_b).reshape(Bn, nc, GMLP_CHUNK, G, Dg)
    mask = jnp.tril(jnp.ones((GMLP_CHUNK, GMLP_CHUNK), dtype=w_s.dtype))
    ws = w_s * mask[None]
    mixed = jnp.einsum('gij,bcjgd->bcigd', ws, vn) + b_s.T[None, None, :, :, None]
    return u * mixed.reshape(Bn, S, G, Dg).astype(u.dtype)


def even_mixer(h, positions, w_in, q_gain, k_gain, w_s, b_s, ln_g, ln_b, w_out):
    Bn, S, _ = h.shape
    proj = h @ w_in
    q, k, v, pu, pv = jnp.split(proj, [A_WIDTH, 2 * A_WIDTH, 3 * A_WIDTH, 3 * A_WIDTH + B_WIDTH], axis=-1)

    def heads(t):
        return t.reshape(Bn, S, A_HEADS, HEAD_DIM).transpose(0, 2, 1, 3)

    q = partial_rope(rmsnorm(heads(q), q_gain), positions)
    k = partial_rope(rmsnorm(heads(k), k_gain), positions)
    a = moba_attention(q, k, heads(v)).transpose(0, 2, 1, 3).reshape(Bn, S, A_WIDTH)
    u = jax.nn.gelu(pu, approximate=False).reshape(Bn, S, B_GROUPS, HEAD_DIM)
    vg = jax.nn.gelu(pv, approximate=False).reshape(Bn, S, B_GROUPS, HEAD_DIM)
    g = chunked_spatial_gating(u, vg, w_s, b_s, ln_g, ln_b).reshape(Bn, S, B_WIDTH)
    return jnp.concatenate([a, g], axis=-1) @ w_out


def gla_chunked(q, k, v, log_a):
    Bn, H, S, dk = q.shape
    dv = v.shape[-1]
    nc = S // GLA_CHUNK
    f32 = jnp.float32
    qc = q.astype(f32).reshape(Bn, H, nc, GLA_CHUNK, dk)
    kc = k.astype(f32).reshape(Bn, H, nc, GLA_CHUNK, dk)
    vc = v.astype(f32).reshape(Bn, H, nc, GLA_CHUNK, dv)
    b = jnp.cumsum(log_a.astype(f32).reshape(Bn, H, nc, GLA_CHUNK, dk), axis=3)
    b_last = b[:, :, :, -1:, :]
    q_in = qc * jnp.exp(b)
    k_in = kc * jnp.exp(-b)
    causal = jnp.tril(jnp.ones((GLA_CHUNK, GLA_CHUNK), dtype=bool))
    att = jnp.where(causal, jnp.einsum('bhnid,bhnjd->bhnij', q_in, k_in), 0.0)
    o_intra = jnp.einsum('bhnij,bhnjv->bhniv', att, vc)
    k_state = kc * jnp.exp(b_last - b)
    d_state = jnp.einsum('bhnjd,bhnjv->bhndv', k_state, vc)
    decay = jnp.exp(b_last[:, :, :, 0, :])

    def step(s_prev, xs):
        ds_n, dec_n = xs
        return dec_n[..., None] * s_prev + ds_n, s_prev

    _, s_before = lax.scan(step, jnp.zeros((Bn, H, dk, dv), f32),
                           (jnp.moveaxis(d_state, 2, 0), jnp.moveaxis(decay, 2, 0)))
    s_before = jnp.moveaxis(s_before, 0, 2)
    o_inter = jnp.einsum('bhnid,bhndv->bhniv', q_in, s_before)
    return (o_intra + o_inter).reshape(Bn, H, S, dv).astype(v.dtype)


def gla_mixer(h, w_in, w_g1, w_g2, b_g, o_gain, w_out):
    Bn, S, _ = h.shape
    qk_w = GLA_HEADS * GLA_DK
    v_w = GLA_HEADS * GLA_DV
    proj = h @ w_in
    q, k, v, r = jnp.split(proj, [qk_w, 2 * qk_w, 2 * qk_w + v_w], axis=-1)
    log_a = jax.nn.log_sigmoid(((h @ w_g1) @ w_g2 + b_g).astype(jnp.float32)) / GLA_TAU

    def heads(t, d):
        return t.reshape(Bn, S, GLA_HEADS, d).transpose(0, 2, 1, 3)

    o = gla_chunked(heads(q, GLA_DK) * (GLA_DK ** -0.5), heads(k, GLA_DK),
                    heads(v, GLA_DV), heads(log_a, GLA_DK))
    o = rmsnorm(o, o_gain[:, None, :])
    o = o.transpose(0, 2, 1, 3).reshape(Bn, S, v_w) * jax.nn.silu(r)
    return o @ w_out


def mem_cross_attention(h, mem_n, w_q, w_kv, q_gain, k_gain, w_out):
    Bn, S, _ = h.shape
    M = mem_n.shape[1]
    q = rmsnorm((h @ w_q).reshape(Bn, S, X_HEADS, X_HEAD_DIM), q_gain)
    k, v = jnp.split(mem_n @ w_kv, 2, axis=-1)
    k = rmsnorm(k.reshape(Bn, M, X_HEADS, X_HEAD_DIM), k_gain)
    v = v.reshape(Bn, M, X_HEADS, X_HEAD_DIM)
    s = jnp.einsum('bqhd,bkhd->bhqk', q, k, preferred_element_type=jnp.float32) * (X_HEAD_DIM ** -0.5)
    p = jax.nn.softmax(s, axis=-1).astype(v.dtype)
    o = jnp.einsum('bhqk,bkhd->bqhd', p, v).reshape(Bn, S, D_MODEL)
    return o @ w_out


def swiglu(h, w_gu, w_down):
    g, u = jnp.split(h @ w_gu, 2, axis=-1)
    return (jax.nn.silu(g) * u) @ w_down


def setup_inputs(seed: int = 0) -> dict:
    key = jax.random.key(seed)
    ks = iter(jax.random.split(key, 32))
    f32 = jnp.float32
    n_ev = (DEPTH + 1) // 2
    n_od = DEPTH // 2

    def nrm(shape, scale):
        return jax.random.normal(next(ks), shape, f32) * scale

    def gain(shape):
        return 1.0 + 0.05 * jax.random.normal(next(ks), shape, f32)

    x = nrm((BATCH, SEQ, D_MODEL), 1.0)
    mem = nrm((BATCH, N_MEM, D_MODEL), 1.0)
    positions = (jnp.arange(SEQ, dtype=jnp.int32)[None, :]
                 + jax.random.randint(next(ks), (BATCH, 1), 0, 4096, dtype=jnp.int32))
    return {
        'x': x,
        'mem': mem,
        'positions': positions,
        'norm_mix': gain((DEPTH, D_MODEL)),
        'norm_mem_q': gain((DEPTH, D_MODEL)),
        'norm_mem_kv': gain((DEPTH, D_MODEL)),
        'norm_ffn': gain((DEPTH, D_MODEL)),
        'ev_w_in': nrm((n_ev, D_MODEL, EV_IN), D_MODEL ** -0.5),
        'ev_q_gain': gain((n_ev, HEAD_DIM)),
        'ev_k_gain': gain((n_ev, HEAD_DIM)),
        'ev_w_s': nrm((n_ev, B_GROUPS, GMLP_CHUNK, GMLP_CHUNK), GMLP_CHUNK ** -0.5),
        'ev_b_s': gain((n_ev, B_GROUPS, GMLP_CHUNK)),
        'ev_ln_g': gain((n_ev, B_GROUPS, HEAD_DIM)),
        'ev_ln_b': nrm((n_ev, B_GROUPS, HEAD_DIM), 0.02),
        'ev_w_out': nrm((n_ev, A_WIDTH + B_WIDTH, D_MODEL), 0.5 * (A_WIDTH + B_WIDTH) ** -0.5),
        'od_w_in': nrm((n_od, D_MODEL, OD_IN), D_MODEL ** -0.5),
        'od_w_g1': nrm((n_od, D_MODEL, GLA_LOWRANK), D_MODEL ** -0.5),
        'od_w_g2': nrm((n_od, GLA_LOWRANK, GLA_HEADS * GLA_DK), GLA_LOWRANK ** -0.5),
        'od_b_g': nrm((n_od, GLA_HEADS * GLA_DK), 0.02),
        'od_o_gain': gain((n_od, GLA_HEADS, GLA_DV)),
        'od_w_out': nrm((n_od, GLA_HEADS * GLA_DV, D_MODEL), 0.5 * (GLA_HEADS * GLA_DV) ** -0.5),
        'xa_w_q': nrm((DEPTH, D_MODEL, D_MODEL), D_MODEL ** -0.5),
        'xa_w_kv': nrm((DEPTH, D_MODEL, 2 * D_MODEL), D_MODEL ** -0.5),
        'xa_q_gain': gain((DEPTH, X_HEAD_DIM)),
        'xa_k_gain': gain((DEPTH, X_HEAD_DIM)),
        'xa_w_out': nrm((DEPTH, D_MODEL, D_MODEL), 0.5 * D_MODEL ** -0.5),
        'ffn_w_gu': nrm((DEPTH, D_MODEL, 2 * D_FF), D_MODEL ** -0.5),
        'ffn_w_down': nrm((DEPTH, D_FF, D_MODEL), 0.5 * D_FF ** -0.5),
    }


def reference(x, mem, positions, norm_mix, norm_mem_q, norm_mem_kv, norm_ffn,
              ev_w_in, ev_q_gain, ev_k_gain, ev_w_s, ev_b_s, ev_ln_g, ev_ln_b, ev_w_out,
              od_w_in, od_w_g1, od_w_g2, od_b_g, od_o_gain, od_w_out,
              xa_w_q, xa_w_kv, xa_q_gain, xa_k_gain, xa_w_out,
              ffn_w_gu, ffn_w_down):
    for l in range(DEPTH):
        i = l // 2
        h = rmsnorm(x, norm_mix[l])
        if l % 2 == 0:
            x = x + even_mixer(h, positions, ev_w_in[i], ev_q_gain[i], ev_k_gain[i],
                               ev_w_s[i], ev_b_s[i], ev_ln_g[i], ev_ln_b[i], ev_w_out[i])
        else:
            x = x + gla_mixer(h, od_w_in[i], od_w_g1[i], od_w_g2[i], od_b_g[i],
                              od_o_gain[i], od_w_out[i])
        x = x + mem_cross_attention(rmsnorm(x, norm_mem_q[l]), rmsnorm(mem, norm_mem_kv[l]),
                                    xa_w_q[l], xa_w_kv[l], xa_q_gain[l], xa_k_gain[l], xa_w_out[l])
        x = x + swiglu(rmsnorm(x, norm_ffn[l]), ffn_w_gu[l], ffn_w_down[l])
    return x
```

```python
import functools
import math

import jax
import jax.numpy as jnp
from jax import lax
from jax.experimental import pallas as pl
from jax.experimental.pallas import tpu as pltpu

F32 = jnp.float32
BF16 = jnp.bfloat16

D_MODEL = 1024
DEPTH = 4
N_MEM = 256
EPS = 1e-6
NEG_INF = -1e30

HEAD_DIM = 64
ROPE_DIM = HEAD_DIM // 4
ROPE_THETA = 500000.0

A_HEADS = 8
A_WIDTH = A_HEADS * HEAD_DIM
MOBA_BLOCK = 256
MOBA_TOPK = 3

B_GROUPS = 8
B_WIDTH = B_GROUPS * HEAD_DIM
GMLP_CHUNK = 128

GLA_HEADS = 4
GLA_DK = (D_MODEL // 2) // GLA_HEADS
GLA_DV = D_MODEL // GLA_HEADS
GLA_LOWRANK = 16
GLA_TAU = 16.0
GLA_CHUNK = 64

X_HEADS = 4
X_HEAD_DIM = D_MODEL // X_HEADS

D_FF = -(-8 * D_MODEL // (3 * 256)) * 256

LANES = 128
VMEM_LIMIT_BYTES = 56 * 1024 * 1024

ROW_TILE = 512
GLA_TILE = 512
FFN_SPLITS = ((0, 1536), (1536, D_FF))


def _params(n_axes):
    return pltpu.CompilerParams(
        dimension_semantics=("arbitrary",) * n_axes,
        vmem_limit_bytes=VMEM_LIMIT_BYTES)


def _const_spec(shape):
    nd = len(shape)
    return pl.BlockSpec(shape, lambda *_: (0,) * nd)


def _dot(a, b):
    return jnp.dot(a, b, preferred_element_type=F32)


def _dot_nt(a, b):
    return lax.dot_general(a, b, (((1,), (1,)), ((), ())),
                           preferred_element_type=F32)


def _dot_tn(a, b):
    return lax.dot_general(a, b, (((0,), (0,)), ((), ())),
                           preferred_element_type=F32)


def _rms_rows(x, gain):
    ms = jnp.mean(x * x, axis=-1, keepdims=True)
    return x * lax.rsqrt(ms + EPS) * gain


def _half_lane_mean(x, lo):
    s_lo = jnp.sum(jnp.where(lo, x, 0.0), axis=-1, keepdims=True)
    s_hi = jnp.sum(jnp.where(lo, 0.0, x), axis=-1, keepdims=True)
    return jnp.where(lo, s_lo, s_hi) * (1.0 / HEAD_DIM)


def _gelu(x):
    return 0.5 * x * (1.0 + lax.erf(x * math.sqrt(0.5)))


def _silu(x):
    return x / (1.0 + jnp.exp(-x))


def _rope_table_kernel(pos_ref, invf_ref, sign_ref, cos_ref, sin_ref):
    ang = pos_ref[...] * invf_ref[...]
    cos_ref[...] = jnp.cos(ang)
    sin_ref[...] = jnp.sin(ang) * sign_ref[...]


def _rope_tables(positions):
    t = positions.size
    rows = 2048
    half = ROPE_DIM // 2
    inv_freq = ROPE_THETA ** (-jnp.arange(half, dtype=F32) * 2.0 / ROPE_DIM)
    d = jnp.arange(LANES) % HEAD_DIM
    invf = jnp.where(d < ROPE_DIM, inv_freq[d % half], 0.0).astype(F32)[None, :]
    sign = jnp.where(d < half, -1.0, jnp.where(d < ROPE_DIM, 1.0, 0.0)).astype(F32)[None, :]
    pos = positions.astype(F32).reshape(t, 1)
    return pl.pallas_call(
        _rope_table_kernel,
        grid=(t // rows,),
        in_specs=[pl.BlockSpec((rows, 1), lambda i: (i, 0)),
                  _const_spec((1, LANES)), _const_spec((1, LANES))],
        out_specs=[pl.BlockSpec((rows, LANES), lambda i: (i, 0))] * 2,
        out_shape=[jax.ShapeDtypeStruct((t, LANES), F32)] * 2,
        compiler_params=_params(1),
        name="rope_tables",
    )(pos, invf, sign)


def _ev_proj_kernel(x_ref, gn_ref, w_ref, qg_ref, kg_ref, cos_ref, sin_ref,
                    lng_ref, lnb_ref, ws_ref, bs_ref,
                    q_out, k_out, v_out, g_out):
    tm = x_ref.shape[0]
    h = _rms_rows(x_ref[...], gn_ref[...]).astype(BF16)
    lane = lax.broadcasted_iota(jnp.int32, (1, LANES), 1)
    lo = lane < HEAD_DIM
    first = (lane % HEAD_DIM) < (ROPE_DIM // 2)
    cos = cos_ref[...]
    sin = sin_ref[...]

    def qk_section(col0, gain_ref, out_ref):
        p = _dot(h, w_ref[:, col0:col0 + A_WIDTH])
        for c in range(A_WIDTH // LANES):
            xc = p[:, c * LANES:(c + 1) * LANES]
            y = xc * lax.rsqrt(_half_lane_mean(xc * xc, lo) + EPS) * gain_ref[...]
            partner = jnp.where(first,
                                pltpu.roll(y, LANES - ROPE_DIM // 2, 1),
                                pltpu.roll(y, ROPE_DIM // 2, 1))
            y = y * cos + partner * sin
            out_ref[:, c * LANES:(c + 1) * LANES] = y.astype(out_ref.dtype)

    qk_section(0, qg_ref, q_out)
    qk_section(A_WIDTH, kg_ref, k_out)
    v_out[...] = _dot(h, w_ref[:, 2 * A_WIDTH:3 * A_WIDTH]).astype(v_out.dtype)

    u = _gelu(_dot(h, w_ref[:, 3 * A_WIDTH:3 * A_WIDTH + B_WIDTH]))
    vg = _gelu(_dot(h, w_ref[:, 3 * A_WIDTH + B_WIDTH:3 * A_WIDTH + 2 * B_WIDTH]))

    row = lax.broadcasted_iota(jnp.int32, (GMLP_CHUNK, GMLP_CHUNK), 0)
    col = lax.broadcasted_iota(jnp.int32, (GMLP_CHUNK, GMLP_CHUNK), 1)
    tril = row >= col
    for c in range(B_WIDTH // LANES):
        sl = slice(c * LANES, (c + 1) * LANES)
        xc = vg[:, sl]
        xc = xc - _half_lane_mean(xc, lo)
        var = _half_lane_mean(xc * xc, lo)
        vn = xc * lax.rsqrt(var + EPS) * lng_ref[:, sl] + lnb_ref[:, sl]
        vn_lo = jnp.where(lo, vn, 0.0).astype(BF16)
        vn_hi = jnp.where(lo, 0.0, vn).astype(BF16)
        w_lo = jnp.where(tril, ws_ref[2 * c], 0.0).astype(BF16)
        w_hi = jnp.where(tril, ws_ref[2 * c + 1], 0.0).astype(BF16)
        bias = bs_ref[:, sl]
        for r in range(tm // GMLP_CHUNK):
            rs = slice(r * GMLP_CHUNK, (r + 1) * GMLP_CHUNK)
            mixed = _dot(w_lo, vn_lo[rs]) + _dot(w_hi, vn_hi[rs]) + bias
            g_out[rs, sl] = (u[rs, sl] * mixed).astype(g_out.dtype)


def _ev_proj(x, gn, w_in, q_gain, k_gain, cos_t, sin_t, ln_g, ln_b, w_s, b_s):
    t = x.shape[0]
    tm = ROW_TILE
    ev_in = w_in.shape[1]
    qg = jnp.tile(q_gain, LANES // HEAD_DIM)[None, :]
    kg = jnp.tile(k_gain, LANES // HEAD_DIM)[None, :]
    lng = ln_g.reshape(1, B_WIDTH)
    lnb = ln_b.reshape(1, B_WIDTH)
    bias = jnp.repeat(b_s.T, HEAD_DIM, axis=1)
    row_spec = lambda n: pl.BlockSpec((tm, n), lambda i: (i, 0))
    return pl.pallas_call(
        _ev_proj_kernel,
        grid=(t // tm,),
        in_specs=[row_spec(D_MODEL), _const_spec((1, D_MODEL)),
                  _const_spec((D_MODEL, ev_in)),
                  _const_spec((1, LANES)), _const_spec((1, LANES)),
                  row_spec(LANES), row_spec(LANES),
                  _const_spec((1, B_WIDTH)), _const_spec((1, B_WIDTH)),
                  _const_spec((B_GROUPS, GMLP_CHUNK, GMLP_CHUNK)),
                  _const_spec((GMLP_CHUNK, B_WIDTH))],
        out_specs=[row_spec(A_WIDTH)] * 3 + [row_spec(B_WIDTH)],
        out_shape=[jax.ShapeDtypeStruct((t, A_WIDTH), BF16)] * 3
                  + [jax.ShapeDtypeStruct((t, B_WIDTH), BF16)],
        compiler_params=_params(1),
        name="ev_proj",
    )(x, gn[None, :], w_in, qg, kg, cos_t, sin_t, lng, lnb, w_s, bias)


def _moba_kernel(q_ref, k_ref, v_ref, eblk_ref, o_ref):
    i = pl.program_id(2)
    seq = k_ref.shape[0]
    nb = seq // MOBA_BLOCK
    q = q_ref[...]
    k = k_ref[...]
    v = v_ref[...]
    scale = HEAD_DIM ** -0.5

    k_mean = jnp.mean(k.astype(F32).reshape(nb, MOBA_BLOCK, LANES), axis=1)
    k_mean = jnp.concatenate(
        [k_mean, jnp.zeros((LANES - nb, LANES), F32)], axis=0).astype(BF16)

    lane = lax.broadcasted_iota(jnp.int32, (1, LANES), 1)
    lo = lane < HEAD_DIM
    blk = lax.broadcasted_iota(jnp.int32, (MOBA_BLOCK, LANES), 1)
    past = blk < i
    qpos = lax.broadcasted_iota(jnp.int32, (MOBA_BLOCK, seq), 0)
    kidx = lax.broadcasted_iota(jnp.int32, (MOBA_BLOCK, seq), 1)
    own = jnp.logical_and(kidx >= i * MOBA_BLOCK, kidx - i * MOBA_BLOCK <= qpos)

    outs = []
    for head_mask in (lo, jnp.logical_not(lo)):
        qh = jnp.where(head_mask, q, jnp.zeros_like(q))
        kmh = jnp.where(head_mask, k_mean, jnp.zeros_like(k_mean))
        gate = jnp.where(past, _dot_nt(qh, kmh), NEG_INF)
        cnt = jnp.zeros(gate.shape, F32)
        for jp in range(nb):
            gj = gate[:, jp:jp + 1]
            ahead = jnp.logical_or(gj > gate,
                                   jnp.logical_and(gj == gate, blk > jp))
            cnt = cnt + jnp.where(ahead, 1.0, 0.0)
        sel = jnp.where(jnp.logical_and(past, cnt < MOBA_TOPK), 1.0, 0.0)
        sel_keys = _dot(sel.astype(BF16), eblk_ref[...])
        allowed = jnp.logical_or(sel_keys > 0.5, own)
        s = _dot_nt(qh, k) * scale
        s = jnp.where(allowed, s, NEG_INF)
        m = jnp.max(s, axis=-1, keepdims=True)
        e = jnp.exp(s - m)
        l = jnp.sum(e, axis=-1, keepdims=True)
        p = (e / l).astype(BF16)
        outs.append(_dot(p, v))
    o_ref[...] = jnp.where(lo, outs[0], outs[1]).astype(o_ref.dtype)


def _moba(q, k, v, batch, seq):
    nb = seq // MOBA_BLOCK
    n_pairs = A_WIDTH // LANES
    eblk = (jnp.arange(LANES)[:, None] == (jnp.arange(seq)[None, :] // MOBA_BLOCK))
    eblk = eblk.astype(BF16)
    q_spec = pl.BlockSpec((MOBA_BLOCK, LANES), lambda b, p, i: (b * nb + i, p))
    kv_spec = pl.BlockSpec((seq, LANES), lambda b, p, i: (b, p))
    return pl.pallas_call(
        _moba_kernel,
        grid=(batch, n_pairs, nb),
        in_specs=[q_spec, kv_spec, kv_spec, _const_spec((LANES, seq))],
        out_specs=q_spec,
        out_shape=jax.ShapeDtypeStruct(q.shape, BF16),
        compiler_params=_params(3),
        name="moba_attn",
    )(q, k, v, eblk)


def _out_proj2_kernel(x_ref, a_ref, g_ref, w_ref, o_ref):
    n_a = a_ref.shape[1]
    o_ref[...] = (x_ref[...] + _dot(a_ref[...], w_ref[:n_a, :])
                  + _dot(g_ref[...], w_ref[n_a:, :]))


def _out_proj2(x, a, g, w_out):
    t = x.shape[0]
    tm = ROW_TILE
    row_spec = lambda n: pl.BlockSpec((tm, n), lambda i: (i, 0))
    return pl.pallas_call(
        _out_proj2_kernel,
        grid=(t // tm,),
        in_specs=[row_spec(D_MODEL), row_spec(a.shape[1]), row_spec(g.shape[1]),
                  _const_spec(w_out.shape)],
        out_specs=row_spec(D_MODEL),
        out_shape=jax.ShapeDtypeStruct(x.shape, F32),
        compiler_params=_params(1),
        name="ev_out_proj",
    )(x, a, g, w_out)


def _out_proj1_kernel(x_ref, a_ref, w_ref, o_ref):
    o_ref[...] = x_ref[...] + _dot(a_ref[...], w_ref[...])


def _out_proj1(x, a, w_out):
    t = x.shape[0]
    tm = ROW_TILE
    row_spec = lambda n: pl.BlockSpec((tm, n), lambda i: (i, 0))
    return pl.pallas_call(
        _out_proj1_kernel,
        grid=(t // tm,),
        in_specs=[row_spec(D_MODEL), row_spec(a.shape[1]), _const_spec(w_out.shape)],
        out_specs=row_spec(D_MODEL),
        out_shape=jax.ShapeDtypeStruct(x.shape, F32),
        compiler_params=_params(1),
        name="od_out_proj",
    )(x, a, w_out)


def _gla_proj_kernel(x_ref, gn_ref, w_ref, wg1_ref, wg2_ref, bg_ref,
                     q_out, k_out, v_out, r_out, la_out):
    qk_w = GLA_HEADS * GLA_DK
    v_w = GLA_HEADS * GLA_DV
    h = _rms_rows(x_ref[...], gn_ref[...]).astype(BF16)
    q_out[...] = _dot(h, w_ref[:, :qk_w]) * (GLA_DK ** -0.5)
    k_out[...] = _dot(h, w_ref[:, qk_w:2 * qk_w])
    v_out[...] = _dot(h, w_ref[:, 2 * qk_w:2 * qk_w + v_w]).astype(v_out.dtype)
    r_out[...] = _dot(h, w_ref[:, 2 * qk_w + v_w:])
    low = _dot(h, wg1_ref[...]).astype(BF16)
    z = _dot(low, wg2_ref[...]) + bg_ref[...]
    log_sig = jnp.minimum(z, 0.0) - jnp.log(1.0 + jnp.exp(-jnp.abs(z)))
    la_out[...] = log_sig * (1.0 / GLA_TAU)


def _gla_proj(x, gn, w_in, w_g1, w_g2, b_g):
    t = x.shape[0]
    tm = ROW_TILE
    qk_w = GLA_HEADS * GLA_DK
    v_w = GLA_HEADS * GLA_DV
    wg1 = jnp.pad(w_g1, ((0, 0), (0, LANES - GLA_LOWRANK)))
    wg2 = jnp.pad(w_g2, ((0, LANES - GLA_LOWRANK), (0, 0)))
    row_spec = lambda n: pl.BlockSpec((tm, n), lambda i: (i, 0))
    return pl.pallas_call(
        _gla_proj_kernel,
        grid=(t // tm,),
        in_specs=[row_spec(D_MODEL), _const_spec((1, D_MODEL)),
                  _const_spec(w_in.shape), _const_spec(wg1.shape),
                  _const_spec(wg2.shape), _const_spec((1, qk_w))],
        out_specs=[row_spec(qk_w), row_spec(qk_w), row_spec(v_w), row_spec(v_w),
                   row_spec(qk_w)],
        out_shape=[jax.ShapeDtypeStruct((t, qk_w), F32),
                   jax.ShapeDtypeStruct((t, qk_w), F32),
                   jax.ShapeDtypeStruct((t, v_w), BF16),
                   jax.ShapeDtypeStruct((t, v_w), F32),
                   jax.ShapeDtypeStruct((t, qk_w), F32)],
        compiler_params=_params(1),
        name="gla_proj",
    )(x, gn[None, :], w_in, wg1, wg2, b_g[None, :])


def _gla_kernel(q_ref, k_ref, v_ref, r_ref, la_ref, og_ref, o_ref, state_ref):
    @pl.when(pl.program_id(2) == 0)
    def _():
        state_ref[...] = jnp.zeros_like(state_ref)

    c = GLA_CHUNK
    row = lax.broadcasted_iota(jnp.int32, (c, c), 0)
    col = lax.broadcasted_iota(jnp.int32, (c, c), 1)
    causal = row >= col
    tri = jnp.where(causal, 1.0, 0.0).astype(F32)
    state = state_ref[...]
    for n in range(q_ref.shape[0] // c):
        rs = slice(n * c, (n + 1) * c)
        b = jnp.dot(tri, la_ref[rs, :], preferred_element_type=F32,
                    precision=lax.Precision.HIGHEST)
        b_last = b[c - 1:c, :]
        qc = q_ref[rs, :]
        kc = k_ref[rs, :]
        vc = v_ref[rs, :]
        q_in = (qc * jnp.exp(b)).astype(BF16)
        k_in = (kc * jnp.exp(-b)).astype(BF16)
        k_state = (kc * jnp.exp(b_last - b)).astype(BF16)
        att = jnp.where(causal, _dot_nt(q_in, k_in), 0.0).astype(BF16)
        o = _dot(att, vc) + _dot_nt(q_in, state.astype(BF16))
        state = jnp.exp(b_last) * state + _dot_tn(vc, k_state)
        o = o * lax.rsqrt(jnp.mean(o * o, axis=-1, keepdims=True) + EPS) * og_ref[...]
        o_ref[rs, :] = (o * _silu(r_ref[rs, :])).astype(o_ref.dtype)
    state_ref[...] = state


def _gla(q, k, v, r, la, o_gain, batch, seq):
    t = q.shape[0]
    tc = GLA_TILE
    steps = seq // tc
    qk_spec = pl.BlockSpec((tc, GLA_DK), lambda b, h, i: (b * steps + i, h))
    v_spec = pl.BlockSpec((tc, GLA_DV), lambda b, h, i: (b * steps + i, h))
    return pl.pallas_call(
        _gla_kernel,
        grid=(batch, GLA_HEADS, steps),
        in_specs=[qk_spec, qk_spec, v_spec, v_spec, qk_spec,
                  pl.BlockSpec((1, GLA_DV), lambda b, h, i: (0, h))],
        out_specs=v_spec,
        out_shape=jax.ShapeDtypeStruct((t, GLA_HEADS * GLA_DV), BF16),
        scratch_shapes=[pltpu.VMEM((GLA_DV, GLA_DK), F32)],
        compiler_params=_params(3),
        name="gla_scan",
    )(q, k, v, r, la, o_gain.reshape(1, GLA_HEADS * GLA_DV))


def _mem_kv_kernel(m_ref, gn_ref, w_ref, kg_ref, k_out, v_out):
    h = _rms_rows(m_ref[...], gn_ref[...]).astype(BF16)
    k = _dot(h, w_ref[:, :D_MODEL])
    for hh in range(X_HEADS):
        sl = slice(hh * X_HEAD_DIM, (hh + 1) * X_HEAD_DIM)
        k_out[:, sl] = _rms_rows(k[:, sl], kg_ref[...]).astype(k_out.dtype)
    v_out[...] = _dot(h, w_ref[:, D_MODEL:]).astype(v_out.dtype)


def _mem_kv(mem, gn, w_kv, k_gain):
    t = mem.shape[0]
    tm = ROW_TILE
    row_spec = pl.BlockSpec((tm, D_MODEL), lambda i: (i, 0))
    return pl.pallas_call(
        _mem_kv_kernel,
        grid=(t // tm,),
        in_specs=[row_spec, _const_spec((1, D_MODEL)), _const_spec(w_kv.shape),
                  _const_spec((1, X_HEAD_DIM))],
        out_specs=[row_spec, row_spec],
        out_shape=[jax.ShapeDtypeStruct((t, D_MODEL), BF16)] * 2,
        compiler_params=_params(1),
        name="mem_kv",
    )(mem, gn[None, :], w_kv, k_gain[None, :])


def _xattn_kernel(x_ref, gn_ref, wq_ref, qg_ref, k_ref, v_ref, wo_ref, o_ref):
    x = x_ref[...]
    h = _rms_rows(x, gn_ref[...]).astype(BF16)
    q = _dot(h, wq_ref[...])
    scale = X_HEAD_DIM ** -0.5
    heads = []
    for hh in range(X_HEADS):
        sl = slice(hh * X_HEAD_DIM, (hh + 1) * X_HEAD_DIM)
        qh = _rms_rows(q[:, sl], qg_ref[...]).astype(BF16)
        s = _dot_nt(qh, k_ref[:, sl]) * scale
        m = jnp.max(s, axis=-1, keepdims=True)
        e = jnp.exp(s - m)
        p = (e / jnp.sum(e, axis=-1, keepdims=True)).astype(BF16)
        heads.append(_dot(p, v_ref[:, sl]).astype(BF16))
    o = jnp.concatenate(heads, axis=-1)
    o_ref[...] = x + _dot(o, wo_ref[...])


def _xattn(x, gn, w_q, q_gain, k_mem, v_mem, w_out, batch, seq):
    tm = ROW_TILE
    steps = seq // tm
    row_spec = pl.BlockSpec((tm, D_MODEL), lambda b, i: (b * steps + i, 0))
    mem_spec = pl.BlockSpec((N_MEM, D_MODEL), lambda b, i: (b, 0))
    return pl.pallas_call(
        _xattn_kernel,
        grid=(batch, steps),
        in_specs=[row_spec, _const_spec((1, D_MODEL)), _const_spec(w_q.shape),
                  _const_spec((1, X_HEAD_DIM)), mem_spec, mem_spec,
                  _const_spec(w_out.shape)],
        out_specs=row_spec,
        out_shape=jax.ShapeDtypeStruct(x.shape, F32),
        compiler_params=_params(2),
        name="mem_xattn",
    )(x, gn[None, :], w_q, q_gain[None, :], k_mem, v_mem, w_out)


def _ffn_kernel(x_ref, gn_ref, wgu_ref, wd_ref, o_ref):
    x = x_ref[...]
    h = _rms_rows(x, gn_ref[...]).astype(BF16)
    acc = x
    for lo, hi in FFN_SPLITS:
        g = _dot(h, wgu_ref[:, lo:hi])
        u = _dot(h, wgu_ref[:, D_FF + lo:D_FF + hi])
        act = (_silu(g) * u).astype(BF16)
        acc = acc + _dot(act, wd_ref[lo:hi, :])
    o_ref[...] = acc


def _ffn(x, gn, w_gu, w_down):
    t = x.shape[0]
    tm = ROW_TILE
    row_spec = pl.BlockSpec((tm, D_MODEL), lambda i: (i, 0))
    return pl.pallas_call(
        _ffn_kernel,
        grid=(t // tm,),
        in_specs=[row_spec, _const_spec((1, D_MODEL)),
                  pl.BlockSpec(w_gu.shape, lambda i: (0, 0), pipeline_mode=pl.Buffered(1)),
                  pl.BlockSpec(w_down.shape, lambda i: (0, 0), pipeline_mode=pl.Buffered(1))],
        out_specs=row_spec,
        out_shape=jax.ShapeDtypeStruct(x.shape, F32),
        compiler_params=_params(1),
        name="swiglu_ffn",
    )(x, gn[None, :], w_gu, w_down)


def kernel(x, mem, positions, norm_mix, norm_mem_q, norm_mem_kv, norm_ffn, ev_w_in, ev_q_gain, ev_k_gain, ev_w_s, ev_b_s, ev_ln_g, ev_ln_b, ev_w_out, od_w_in, od_w_g1, od_w_g2, od_b_g, od_o_gain, od_w_out, xa_w_q, xa_w_kv, xa_q_gain, xa_k_gain, xa_w_out, ffn_w_gu, ffn_w_down):
    batch, seq, d_model = x.shape
    assert d_model == D_MODEL and seq % ROW_TILE == 0 and seq % MOBA_BLOCK == 0
    assert mem.shape == (batch, N_MEM, D_MODEL)
    xf = x.reshape(batch * seq, D_MODEL)
    memf = mem.reshape(batch * N_MEM, D_MODEL)
    cos_t, sin_t = _rope_tables(positions)
    bf = lambda w: w.astype(BF16)
    for l in range(DEPTH):
        i = l // 2
        if l % 2 == 0:
            q, k, v, g = _ev_proj(xf, norm_mix[l], bf(ev_w_in[i]), ev_q_gain[i],
                                  ev_k_gain[i], cos_t, sin_t, ev_ln_g[i], ev_ln_b[i],
                                  ev_w_s[i], ev_b_s[i])
            a = _moba(q, k, v, batch, seq)
            xf = _out_proj2(xf, a, g, bf(ev_w_out[i]))
        else:
            q, k, v, r, la = _gla_proj(xf, norm_mix[l], bf(od_w_in[i]), bf(od_w_g1[i]),
                                       bf(od_w_g2[i]), od_b_g[i])
            o = _gla(q, k, v, r, la, od_o_gain[i], batch, seq)
            xf = _out_proj1(xf, o, bf(od_w_out[i]))
        k_mem, v_mem = _mem_kv(memf, norm_mem_kv[l], bf(xa_w_kv[l]), xa_k_gain[l])
        xf = _xattn(xf, norm_mem_q[l], bf(xa_w_q[l]), xa_q_gain[l], k_mem, v_mem,
                    bf(xa_w_out[l]), batch, seq)
        xf = _ffn(xf, norm_ffn[l], bf(ffn_w_gu[l]), bf(ffn_w_down[l]))
    return xf.reshape(batch, seq, D_MODEL)
```

```python
import functools
import math

import jax
import jax.numpy as jnp
from jax import lax
from jax.experimental import pallas as pl
from jax.experimental.pallas import tpu as pltpu

F32 = jnp.float32
BF16 = jnp.bfloat16

D_MODEL = 1024
DEPTH = 4
N_MEM = 256
EPS = 1e-6
NEG_INF = -1e30

HEAD_DIM = 64
ROPE_DIM = HEAD_DIM // 4
ROPE_THETA = 500000.0

A_HEADS = 8
A_WIDTH = A_HEADS * HEAD_DIM
MOBA_BLOCK = 256
MOBA_TOPK = 3

B_GROUPS = 8
B_WIDTH = B_GROUPS * HEAD_DIM
GMLP_CHUNK = 128

GLA_HEADS = 4
GLA_DK = (D_MODEL // 2) // GLA_HEADS
GLA_DV = D_MODEL // GLA_HEADS
GLA_LOWRANK = 16
GLA_TAU = 16.0
GLA_CHUNK = 64

X_HEADS = 4
X_HEAD_DIM = D_MODEL // X_HEADS

D_FF = -(-8 * D_MODEL // (3 * 256)) * 256

LANES = 128
VMEM_LIMIT_BYTES = 56 * 1024 * 1024

ROW_TILE = 512
GLA_TILE = 512
FFN_SPLITS = ((0, 1536), (1536, D_FF))


def _params(n_axes):
    return pltpu.CompilerParams(
        dimension_semantics=("arbitrary",) * n_axes,
        vmem_limit_bytes=VMEM_LIMIT_BYTES)


def _const_spec(shape):
    nd = len(shape)
    return pl.BlockSpec(shape, lambda *_: (0,) * nd)


def _dot(a, b):
    return jnp.dot(a, b, preferred_element_type=F32)


def _dot_nt(a, b):
    return lax.dot_general(a, b, (((1,), (1,)), ((), ())),
                           preferred_element_type=F32)


def _dot_tn(a, b):
    return lax.dot_general(a, b, (((0,), (0,)), ((), ())),
                           preferred_element_type=F32)


def _rms_rows(x, gain):
    ms = jnp.mean(x * x, axis=-1, keepdims=True)
    return x * lax.rsqrt(ms + EPS) * gain


def _half_lane_mean(x, lo):
    s_lo = jnp.sum(jnp.where(lo, x, 0.0), axis=-1, keepdims=True)
    s_hi = jnp.sum(jnp.where(lo, 0.0, x), axis=-1, keepdims=True)
    return jnp.where(lo, s_lo, s_hi) * (1.0 / HEAD_DIM)


def _gelu(x):
    return 0.5 * x * (1.0 + lax.erf(x * math.sqrt(0.5)))


def _silu(x):
    return x / (1.0 + jnp.exp(-x))


def _rope_table_kernel(pos_ref, invf_ref, sign_ref, cos_ref, sin_ref):
    ang = pos_ref[...] * invf_ref[...]
    cos_ref[...] = jnp.cos(ang)
    sin_ref[...] = jnp.sin(ang) * sign_ref[...]


def _rope_tables(positions):
    t = positions.size
    rows = 2048
    half = ROPE_DIM // 2
    inv_freq = ROPE_THETA ** (-jnp.arange(half, dtype=F32) * 2.0 / ROPE_DIM)
    d = jnp.arange(LANES) % HEAD_DIM
    invf = jnp.where(d < ROPE_DIM, inv_freq[d % half], 0.0).astype(F32)[None, :]
    sign = jnp.where(d < half, -1.0, jnp.where(d < ROPE_DIM, 1.0, 0.0)).astype(F32)[None, :]
    pos = positions.astype(F32).reshape(t, 1)
    return pl.pallas_call(
        _rope_table_kernel,
        grid=(t // rows,),
        in_specs=[pl.BlockSpec((rows, 1), lambda i: (i, 0)),
                  _const_spec((1, LANES)), _const_spec((1, LANES))],
        out_specs=[pl.BlockSpec((rows, LANES), lambda i: (i, 0))] * 2,
        out_shape=[jax.ShapeDtypeStruct((t, LANES), F32)] * 2,
        compiler_params=_params(1),
        name="rope_tables",
    )(pos, invf, sign)


def _ev_proj_kernel(x_ref, gn_ref, w_ref, wvt_ref, qg_ref, kg_ref, cos_ref, sin_ref,
                    lng_ref, lnb_ref, ws_ref, bs_ref,
                    q_out, k_out, vt_out, g_out):
    tm = x_ref.shape[0]
    h = _rms_rows(x_ref[...], gn_ref[...]).astype(BF16)
    lane = lax.broadcasted_iota(jnp.int32, (1, LANES), 1)
    lo = lane < HEAD_DIM
    first = (lane % HEAD_DIM) < (ROPE_DIM // 2)
    cos = cos_ref[...]
    sin = sin_ref[...]

    def qk_section(col0, gain_ref, out_ref):
        p = _dot(h, w_ref[:, col0:col0 + A_WIDTH])
        for c in range(A_WIDTH // LANES):
            xc = p[:, c * LANES:(c + 1) * LANES]
            y = xc * lax.rsqrt(_half_lane_mean(xc * xc, lo) + EPS) * gain_ref[...]
            partner = jnp.where(first,
                                pltpu.roll(y, LANES - ROPE_DIM // 2, 1),
                                pltpu.roll(y, ROPE_DIM // 2, 1))
            y = y * cos + partner * sin
            out_ref[:, c * LANES:(c + 1) * LANES] = y.astype(out_ref.dtype)

    qk_section(0, qg_ref, q_out)
    qk_section(A_WIDTH, kg_ref, k_out)
    vt_out[...] = _dot_nt(wvt_ref[...], h).astype(vt_out.dtype)

    u = _gelu(_dot(h, w_ref[:, 3 * A_WIDTH:3 * A_WIDTH + B_WIDTH]))
    vg = _gelu(_dot(h, w_ref[:, 3 * A_WIDTH + B_WIDTH:3 * A_WIDTH + 2 * B_WIDTH]))

    row = lax.broadcasted_iota(jnp.int32, (GMLP_CHUNK, GMLP_CHUNK), 0)
    col = lax.broadcasted_iota(jnp.int32, (GMLP_CHUNK, GMLP_CHUNK), 1)
    tril = row >= col
    for c in range(B_WIDTH // LANES):
        sl = slice(c * LANES, (c + 1) * LANES)
        xc = vg[:, sl]
        xc = xc - _half_lane_mean(xc, lo)
        var = _half_lane_mean(xc * xc, lo)
        vn = xc * lax.rsqrt(var + EPS) * lng_ref[:, sl] + lnb_ref[:, sl]
        vn_lo = jnp.where(lo, vn, 0.0).astype(BF16)
        vn_hi = jnp.where(lo, 0.0, vn).astype(BF16)
        w_lo = jnp.where(tril, ws_ref[2 * c], 0.0).astype(BF16)
        w_hi = jnp.where(tril, ws_ref[2 * c + 1], 0.0).astype(BF16)
        bias = bs_ref[:, sl]
        for r in range(tm // GMLP_CHUNK):
            rs = slice(r * GMLP_CHUNK, (r + 1) * GMLP_CHUNK)
            mixed = _dot(w_lo, vn_lo[rs]) + _dot(w_hi, vn_hi[rs]) + bias
            g_out[rs, sl] = (u[rs, sl] * mixed).astype(g_out.dtype)


def _ev_proj(x, gn, w_in, q_gain, k_gain, cos_t, sin_t, ln_g, ln_b, w_s, b_s):
    t = x.shape[0]
    tm = ROW_TILE
    ev_in = w_in.shape[1]
    w_v_t = w_in[:, 2 * A_WIDTH:3 * A_WIDTH].T
    qg = jnp.tile(q_gain, LANES // HEAD_DIM)[None, :]
    kg = jnp.tile(k_gain, LANES // HEAD_DIM)[None, :]
    lng = ln_g.reshape(1, B_WIDTH)
    lnb = ln_b.reshape(1, B_WIDTH)
    bias = jnp.repeat(b_s.T, HEAD_DIM, axis=1)
    row_spec = lambda n: pl.BlockSpec((tm, n), lambda i: (i, 0))
    return pl.pallas_call(
        _ev_proj_kernel,
        grid=(t // tm,),
        in_specs=[row_spec(D_MODEL), _const_spec((1, D_MODEL)),
                  _const_spec((D_MODEL, ev_in)), _const_spec((A_WIDTH, D_MODEL)),
                  _const_spec((1, LANES)), _const_spec((1, LANES)),
                  row_spec(LANES), row_spec(LANES),
                  _const_spec((1, B_WIDTH)), _const_spec((1, B_WIDTH)),
                  _const_spec((B_GROUPS, GMLP_CHUNK, GMLP_CHUNK)),
                  _const_spec((GMLP_CHUNK, B_WIDTH))],
        out_specs=[row_spec(A_WIDTH), row_spec(A_WIDTH),
                   pl.BlockSpec((A_WIDTH, tm), lambda i: (0, i)), row_spec(B_WIDTH)],
        out_shape=[jax.ShapeDtypeStruct((t, A_WIDTH), BF16),
                   jax.ShapeDtypeStruct((t, A_WIDTH), BF16),
                   jax.ShapeDtypeStruct((A_WIDTH, t), BF16),
                   jax.ShapeDtypeStruct((t, B_WIDTH), BF16)],
        compiler_params=_params(1),
        name="ev_proj",
    )(x, gn[None, :], w_in, w_v_t, qg, kg, cos_t, sin_t, lng, lnb, w_s, bias)


def _moba_kernel(q_ref, k_ref, vt_ref, o_ref):
    seq = k_ref.shape[0]
    nb = seq // MOBA_BLOCK
    blk = MOBA_BLOCK
    scale = HEAD_DIM ** -0.5

    k_mean = jnp.mean(k_ref[...].astype(F32).reshape(nb, blk, LANES), axis=1)
    k_mean = jnp.concatenate([k_mean, jnp.zeros((16 - nb, LANES), F32)], axis=0).astype(BF16)

    lane = lax.broadcasted_iota(jnp.int32, (1, LANES), 1)
    lo = lane < HEAD_DIM
    key_i = lax.broadcasted_iota(jnp.int32, (blk, blk), 0)
    qry_i = lax.broadcasted_iota(jnp.int32, (blk, blk), 1)
    causal = key_i <= qry_i

    for i in range(nb):
        q_i = q_ref[i * blk:(i + 1) * blk, :]
        head_out = []
        for hd in range(LANES // HEAD_DIM):
            head_mask = lo if hd == 0 else jnp.logical_not(lo)
            qh = jnp.where(head_mask, q_i, jnp.zeros_like(q_i)) * scale
            s = [_dot_nt(k_ref[j * blk:(j + 1) * blk, :], qh) for j in range(i + 1)]
            s[i] = jnp.where(causal, s[i], NEG_INF)
            col_max = [jnp.max(sj, axis=0, keepdims=True) for sj in s]
            if i > MOBA_TOPK:
                gate = _dot_nt(k_mean, qh)
                g = [gate[j:j + 1, :] for j in range(i)]
                bias = []
                for j in range(i):
                    cnt = jnp.zeros_like(g[j])
                    for jp in range(i):
                        if jp == j:
                            continue
                        ahead = (g[jp] >= g[j]) if jp < j else (g[jp] > g[j])
                        cnt = cnt + jnp.where(ahead, 1.0, 0.0)
                    bias.append(jnp.where(cnt < MOBA_TOPK, 0.0, NEG_INF))
            else:
                bias = [None] * i
            m = col_max[i]
            for j in range(i):
                m = jnp.maximum(m, col_max[j] if bias[j] is None else col_max[j] + bias[j])
            l = jnp.zeros_like(m)
            acc = jnp.zeros((HEAD_DIM, blk), F32)
            for j in range(i + 1):
                shift = -m if (j == i or bias[j] is None) else bias[j] - m
                p = jnp.exp(s[j] + shift)
                l = l + jnp.sum(p, axis=0, keepdims=True)
                acc = acc + _dot(vt_ref[hd * HEAD_DIM:(hd + 1) * HEAD_DIM, j * blk:(j + 1) * blk],
                                 p.astype(BF16))
            head_out.append(acc / l)
        o_t = jnp.concatenate(head_out, axis=0)
        o_ref[i * blk:(i + 1) * blk, :] = o_t.T.astype(o_ref.dtype)


def _moba(q, k, v_t, batch, seq):
    n_pairs = A_WIDTH // LANES
    qk_spec = pl.BlockSpec((seq, LANES), lambda b, p: (b, p))
    vt_spec = pl.BlockSpec((LANES, seq), lambda b, p: (p, b))
    return pl.pallas_call(
        _moba_kernel,
        grid=(batch, n_pairs),
        in_specs=[qk_spec, qk_spec, vt_spec],
        out_specs=qk_spec,
        out_shape=jax.ShapeDtypeStruct(q.shape, BF16),
        compiler_params=_params(2),
        name="moba_attn",
    )(q, k, v_t)


def _out_proj2_kernel(x_ref, a_ref, g_ref, w_ref, o_ref):
    n_a = a_ref.shape[1]
    o_ref[...] = (x_ref[...] + _dot(a_ref[...], w_ref[:n_a, :])
                  + _dot(g_ref[...], w_ref[n_a:, :]))


def _out_proj2(x, a, g, w_out):
    t = x.shape[0]
    tm = ROW_TILE
    row_spec = lambda n: pl.BlockSpec((tm, n), lambda i: (i, 0))
    return pl.pallas_call(
        _out_proj2_kernel,
        grid=(t // tm,),
        in_specs=[row_spec(D_MODEL), row_spec(a.shape[1]), row_spec(g.shape[1]),
                  _const_spec(w_out.shape)],
        out_specs=row_spec(D_MODEL),
        out_shape=jax.ShapeDtypeStruct(x.shape, F32),
        compiler_params=_params(1),
        name="ev_out_proj",
    )(x, a, g, w_out)


def _out_proj1_kernel(x_ref, a_ref, w_ref, o_ref):
    o_ref[...] = x_ref[...] + _dot(a_ref[...], w_ref[...])


def _out_proj1(x, a, w_out):
    t = x.shape[0]
    tm = ROW_TILE
    row_spec = lambda n: pl.BlockSpec((tm, n), lambda i: (i, 0))
    return pl.pallas_call(
        _out_proj1_kernel,
        grid=(t // tm,),
        in_specs=[row_spec(D_MODEL), row_spec(a.shape[1]), _const_spec(w_out.shape)],
        out_specs=row_spec(D_MODEL),
        out_shape=jax.ShapeDtypeStruct(x.shape, F32),
        compiler_params=_params(1),
        name="od_out_proj",
    )(x, a, w_out)


def _gla_proj_kernel(x_ref, gn_ref, w_ref, wg1_ref, wg2_ref, bg_ref,
                     q_out, k_out, v_out, r_out, la_out):
    qk_w = GLA_HEADS * GLA_DK
    v_w = GLA_HEADS * GLA_DV
    h = _rms_rows(x_ref[...], gn_ref[...]).astype(BF16)
    q_out[...] = _dot(h, w_ref[:, :qk_w]) * (GLA_DK ** -0.5)
    k_out[...] = _dot(h, w_ref[:, qk_w:2 * qk_w])
    v_out[...] = _dot(h, w_ref[:, 2 * qk_w:2 * qk_w + v_w]).astype(v_out.dtype)
    r_out[...] = _dot(h, w_ref[:, 2 * qk_w + v_w:])
    low = _dot(h, wg1_ref[...]).astype(BF16)
    z = _dot(low, wg2_ref[...]) + bg_ref[...]
    log_sig = jnp.minimum(z, 0.0) - jnp.log(1.0 + jnp.exp(-jnp.abs(z)))
    la_out[...] = log_sig * (1.0 / GLA_TAU)


def _gla_proj(x, gn, w_in, w_g1, w_g2, b_g):
    t = x.shape[0]
    tm = ROW_TILE
    qk_w = GLA_HEADS * GLA_DK
    v_w = GLA_HEADS * GLA_DV
    wg1 = jnp.pad(w_g1, ((0, 0), (0, LANES - GLA_LOWRANK)))
    wg2 = jnp.pad(w_g2, ((0, LANES - GLA_LOWRANK), (0, 0)))
    row_spec = lambda n: pl.BlockSpec((tm, n), lambda i: (i, 0))
    return pl.pallas_call(
        _gla_proj_kernel,
        grid=(t // tm,),
        in_specs=[row_spec(D_MODEL), _const_spec((1, D_MODEL)),
                  _const_spec(w_in.shape), _const_spec(wg1.shape),
                  _const_spec(wg2.shape), _const_spec((1, qk_w))],
        out_specs=[row_spec(qk_w), row_spec(qk_w), row_spec(v_w), row_spec(v_w),
                   row_spec(qk_w)],
        out_shape=[jax.ShapeDtypeStruct((t, qk_w), F32),
                   jax.ShapeDtypeStruct((t, qk_w), F32),
                   jax.ShapeDtypeStruct((t, v_w), BF16),
                   jax.ShapeDtypeStruct((t, v_w), F32),
                   jax.ShapeDtypeStruct((t, qk_w), F32)],
        compiler_params=_params(1),
        name="gla_proj",
    )(x, gn[None, :], w_in, wg1, wg2, b_g[None, :])


def _gla_kernel(q_ref, k_ref, v_ref, r_ref, la_ref, og_ref, o_ref, state_ref):
    @pl.when(pl.program_id(2) == 0)
    def _():
        state_ref[...] = jnp.zeros_like(state_ref)

    c = GLA_CHUNK
    row = lax.broadcasted_iota(jnp.int32, (c, c), 0)
    col = lax.broadcasted_iota(jnp.int32, (c, c), 1)
    causal = row >= col
    tri = jnp.where(causal, 1.0, 0.0).astype(F32)
    state = state_ref[...]
    for n in range(q_ref.shape[0] // c):
        rs = slice(n * c, (n + 1) * c)
        b = jnp.dot(tri, la_ref[rs, :], preferred_element_type=F32,
                    precision=lax.Precision.HIGHEST)
        b_last = b[c - 1:c, :]
        qc = q_ref[rs, :]
        kc = k_ref[rs, :]
        vc = v_ref[rs, :]
        q_in = (qc * jnp.exp(b)).astype(BF16)
        k_in = (kc * jnp.exp(-b)).astype(BF16)
        k_state = (kc * jnp.exp(b_last - b)).astype(BF16)
        att = jnp.where(causal, _dot_nt(q_in, k_in), 0.0).astype(BF16)
        o = _dot(att, vc) + _dot_nt(q_in, state.astype(BF16))
        state = jnp.exp(b_last) * state + _dot_tn(vc, k_state)
        o = o * lax.rsqrt(jnp.mean(o * o, axis=-1, keepdims=True) + EPS) * og_ref[...]
        o_ref[rs, :] = (o * _silu(r_ref[rs, :])).astype(o_ref.dtype)
    state_ref[...] = state


def _gla(q, k, v, r, la, o_gain, batch, seq):
    t = q.shape[0]
    tc = GLA_TILE
    steps = seq // tc
    qk_spec = pl.BlockSpec((tc, GLA_DK), lambda b, h, i: (b * steps + i, h))
    v_spec = pl.BlockSpec((tc, GLA_DV), lambda b, h, i: (b * steps + i, h))
    return pl.pallas_call(
        _gla_kernel,
        grid=(batch, GLA_HEADS, steps),
        in_specs=[qk_spec, qk_spec, v_spec, v_spec, qk_spec,
                  pl.BlockSpec((1, GLA_DV), lambda b, h, i: (0, h))],
        out_specs=v_spec,
        out_shape=jax.ShapeDtypeStruct((t, GLA_HEADS * GLA_DV), BF16),
        scratch_shapes=[pltpu.VMEM((GLA_DV, GLA_DK), F32)],
        compiler_params=_params(3),
        name="gla_scan",
    )(q, k, v, r, la, o_gain.reshape(1, GLA_HEADS * GLA_DV))


def _mem_kv_kernel(m_ref, gn_ref, w_ref, kg_ref, k_out, v_out):
    h = _rms_rows(m_ref[...], gn_ref[...]).astype(BF16)
    k = _dot(h, w_ref[:, :D_MODEL])
    for hh in range(X_HEADS):
        sl = slice(hh * X_HEAD_DIM, (hh + 1) * X_HEAD_DIM)
        k_out[:, sl] = _rms_rows(k[:, sl], kg_ref[...]).astype(k_out.dtype)
    v_out[...] = _dot(h, w_ref[:, D_MODEL:]).astype(v_out.dtype)


def _mem_kv(mem, gn, w_kv, k_gain):
    t = mem.shape[0]
    tm = ROW_TILE
    row_spec = pl.BlockSpec((tm, D_MODEL), lambda i: (i, 0))
    return pl.pallas_call(
        _mem_kv_kernel,
        grid=(t // tm,),
        in_specs=[row_spec, _const_spec((1, D_MODEL)), _const_spec(w_kv.shape),
                  _const_spec((1, X_HEAD_DIM))],
        out_specs=[row_spec, row_spec],
        out_shape=[jax.ShapeDtypeStruct((t, D_MODEL), BF16)] * 2,
        compiler_params=_params(1),
        name="mem_kv",
    )(mem, gn[None, :], w_kv, k_gain[None, :])


def _xattn_kernel(x_ref, gn_ref, wq_ref, qg_ref, k_ref, v_ref, wo_ref, o_ref):
    x = x_ref[...]
    h = _rms_rows(x, gn_ref[...]).astype(BF16)
    q = _dot(h, wq_ref[...])
    scale = X_HEAD_DIM ** -0.5
    heads = []
    for hh in range(X_HEADS):
        sl = slice(hh * X_HEAD_DIM, (hh + 1) * X_HEAD_DIM)
        qh = _rms_rows(q[:, sl], qg_ref[...]).astype(BF16)
        s = _dot_nt(qh, k_ref[:, sl]) * scale
        m = jnp.max(s, axis=-1, keepdims=True)
        e = jnp.exp(s - m)
        p = (e / jnp.sum(e, axis=-1, keepdims=True)).astype(BF16)
        heads.append(_dot(p, v_ref[:, sl]).astype(BF16))
    o = jnp.concatenate(heads, axis=-1)
    o_ref[...] = x + _dot(o, wo_ref[...])


def _xattn(x, gn, w_q, q_gain, k_mem, v_mem, w_out, batch, seq):
    tm = ROW_TILE
    steps = seq // tm
    row_spec = pl.BlockSpec((tm, D_MODEL), lambda b, i: (b * steps + i, 0))
    mem_spec = pl.BlockSpec((N_MEM, D_MODEL), lambda b, i: (b, 0))
    return pl.pallas_call(
        _xattn_kernel,
        grid=(batch, steps),
        in_specs=[row_spec, _const_spec((1, D_MODEL)), _const_spec(w_q.shape),
                  _const_spec((1, X_HEAD_DIM)), mem_spec, mem_spec,
                  _const_spec(w_out.shape)],
        out_specs=row_spec,
        out_shape=jax.ShapeDtypeStruct(x.shape, F32),
        compiler_params=_params(2),
        name="mem_xattn",
    )(x, gn[None, :], w_q, q_gain[None, :], k_mem, v_mem, w_out)


def _ffn_kernel(x_ref, gn_ref, wgu_ref, wd_ref, o_ref):
    x = x_ref[...]
    h = _rms_rows(x, gn_ref[...]).astype(BF16)
    acc = x
    for lo, hi in FFN_SPLITS:
        g = _dot(h, wgu_ref[:, lo:hi])
        u = _dot(h, wgu_ref[:, D_FF + lo:D_FF + hi])
        act = (_silu(g) * u).astype(BF16)
        acc = acc + _dot(act, wd_ref[lo:hi, :])
    o_ref[...] = acc


def _ffn(x, gn, w_gu, w_down):
    t = x.shape[0]
    tm = ROW_TILE
    row_spec = pl.BlockSpec((tm, D_MODEL), lambda i: (i, 0))
    return pl.pallas_call(
        _ffn_kernel,
        grid=(t // tm,),
        in_specs=[row_spec, _const_spec((1, D_MODEL)),
                  pl.BlockSpec(w_gu.shape, lambda i: (0, 0), pipeline_mode=pl.Buffered(1)),
                  pl.BlockSpec(w_down.shape, lambda i: (0, 0), pipeline_mode=pl.Buffered(1))],
        out_specs=row_spec,
        out_shape=jax.ShapeDtypeStruct(x.shape, F32),
        compiler_params=_params(1),
        name="swiglu_ffn",
    )(x, gn[None, :], w_gu, w_down)


def kernel(x, mem, positions, norm_mix, norm_mem_q, norm_mem_kv, norm_ffn, ev_w_in, ev_q_gain, ev_k_gain, ev_w_s, ev_b_s, ev_ln_g, ev_ln_b, ev_w_out, od_w_in, od_w_g1, od_w_g2, od_b_g, od_o_gain, od_w_out, xa_w_q, xa_w_kv, xa_q_gain, xa_k_gain, xa_w_out, ffn_w_gu, ffn_w_down):
    batch, seq, d_model = x.shape
    assert d_model == D_MODEL and seq % ROW_TILE == 0 and seq % MOBA_BLOCK == 0
    assert mem.shape == (batch, N_MEM, D_MODEL)
    xf = x.reshape(batch * seq, D_MODEL)
    memf = mem.reshape(batch * N_MEM, D_MODEL)
    cos_t, sin_t = _rope_tables(positions)
    bf = lambda w: w.astype(BF16)
    for l in range(DEPTH):
        i = l // 2
        if l % 2 == 0:
            q, k, v_t, g = _ev_proj(xf, norm_mix[l], bf(ev_w_in[i]), ev_q_gain[i],
                                  ev_k_gain[i], cos_t, sin_t, ev_ln_g[i], ev_ln_b[i],
                                  ev_w_s[i], ev_b_s[i])
            a = _moba(q, k, v_t, batch, seq)
            xf = _out_proj2(xf, a, g, bf(ev_w_out[i]))
        else:
            q, k, v, r, la = _gla_proj(xf, norm_mix[l], bf(od_w_in[i]), bf(od_w_g1[i]),
                                       bf(od_w_g2[i]), od_b_g[i])
            o = _gla(q, k, v, r, la, od_o_gain[i], batch, seq)
            xf = _out_proj1(xf, o, bf(od_w_out[i]))
        k_mem, v_mem = _mem_kv(memf, norm_mem_kv[l], bf(xa_w_kv[l]), xa_k_gain[l])
        xf = _xattn(xf, norm_mem_q[l], bf(xa_w_q[l]), xa_q_gain[l], k_mem, v_mem,
                    bf(xa_w_out[l]), batch, seq)
        xf = _ffn(xf, norm_ffn[l], bf(ffn_w_gu[l]), bf(ffn_w_down[l]))
    return xf.reshape(batch, seq, D_MODEL)
```

```python
import functools
import math

import jax
import jax.numpy as jnp
from jax import lax
from jax.experimental import pallas as pl
from jax.experimental.pallas import tpu as pltpu

F32 = jnp.float32
BF16 = jnp.bfloat16

D_MODEL = 1024
DEPTH = 4
N_MEM = 256
EPS = 1e-6
NEG_INF = -1e30

HEAD_DIM = 64
ROPE_DIM = HEAD_DIM // 4
ROPE_THETA = 500000.0

A_HEADS = 8
A_WIDTH = A_HEADS * HEAD_DIM
MOBA_BLOCK = 256
MOBA_TOPK = 3

B_GROUPS = 8
B_WIDTH = B_GROUPS * HEAD_DIM
GMLP_CHUNK = 128

GLA_HEADS = 4
GLA_DK = (D_MODEL // 2) // GLA_HEADS
GLA_DV = D_MODEL // GLA_HEADS
GLA_QK_W = GLA_HEADS * GLA_DK
GLA_V_W = GLA_HEADS * GLA_DV
GLA_LOWRANK = 16
GLA_TAU = 16.0
GLA_CHUNK = 64

X_HEADS = 4
X_HEAD_DIM = D_MODEL // X_HEADS

D_FF = -(-8 * D_MODEL // (3 * 256)) * 256

LANES = 128
VMEM_LIMIT_BYTES = 56 * 1024 * 1024

ROW_TILE = 512
FFN_SPLITS = ((0, 1536), (1536, D_FF))


def _params(n_axes):
    return pltpu.CompilerParams(
        dimension_semantics=("arbitrary",) * n_axes,
        vmem_limit_bytes=VMEM_LIMIT_BYTES)


def _const_spec(shape):
    nd = len(shape)
    return pl.BlockSpec(shape, lambda *_: (0,) * nd, pipeline_mode=pl.Buffered(1))


def _dot(a, b):
    return jnp.dot(a, b, preferred_element_type=F32)


def _dot_nt(a, b):
    return lax.dot_general(a, b, (((1,), (1,)), ((), ())),
                           preferred_element_type=F32)


def _dot_tn(a, b):
    return lax.dot_general(a, b, (((0,), (0,)), ((), ())),
                           preferred_element_type=F32)


def _rms_rows(x, gain):
    ms = jnp.mean(x * x, axis=-1, keepdims=True)
    return x * lax.rsqrt(ms + EPS) * gain


def _half_lane_mean(x, lo):
    s_lo = jnp.sum(jnp.where(lo, x, 0.0), axis=-1, keepdims=True)
    s_hi = jnp.sum(jnp.where(lo, 0.0, x), axis=-1, keepdims=True)
    return jnp.where(lo, s_lo, s_hi) * (1.0 / HEAD_DIM)


def _gelu(x):
    return 0.5 * x * (1.0 + lax.erf(x * math.sqrt(0.5)))


def _silu(x):
    return x / (1.0 + jnp.exp(-x))


def _rope_table_kernel(pos_ref, invf_ref, sign_ref, cos_ref, sin_ref):
    ang = pos_ref[...] * invf_ref[...]
    cos_ref[...] = jnp.cos(ang)
    sin_ref[...] = jnp.sin(ang) * sign_ref[...]


def _rope_tables(positions):
    t = positions.size
    rows = 2048
    half = ROPE_DIM // 2
    inv_freq = ROPE_THETA ** (-jnp.arange(half, dtype=F32) * 2.0 / ROPE_DIM)
    d = jnp.arange(LANES) % HEAD_DIM
    invf = jnp.where(d < ROPE_DIM, inv_freq[d % half], 0.0).astype(F32)[None, :]
    sign = jnp.where(d < half, -1.0, jnp.where(d < ROPE_DIM, 1.0, 0.0)).astype(F32)[None, :]
    pos = positions.astype(F32).reshape(t, 1)
    return pl.pallas_call(
        _rope_table_kernel,
        grid=(t // rows,),
        in_specs=[pl.BlockSpec((rows, 1), lambda i: (i, 0)),
                  _const_spec((1, LANES)), _const_spec((1, LANES))],
        out_specs=[pl.BlockSpec((rows, LANES), lambda i: (i, 0))] * 2,
        out_shape=[jax.ShapeDtypeStruct((t, LANES), F32)] * 2,
        compiler_params=_params(1),
        name="rope_tables",
    )(pos, invf, sign)


def _ev_proj_kernel(x_ref, gn_ref, w_ref, wvt_ref, qg_ref, kg_ref, cos_ref, sin_ref,
                    lng_ref, lnb_ref, ws_ref, bs_ref,
                    q_out, k_out, vt_out, g_out):
    tm = x_ref.shape[0]
    h = _rms_rows(x_ref[...], gn_ref[...]).astype(BF16)
    lane = lax.broadcasted_iota(jnp.int32, (1, LANES), 1)
    lo = lane < HEAD_DIM
    first = (lane % HEAD_DIM) < (ROPE_DIM // 2)
    cos = cos_ref[...]
    sin = sin_ref[...]

    def qk_section(col0, gain_ref, out_ref):
        p = _dot(h, w_ref[:, col0:col0 + A_WIDTH])
        for c in range(A_WIDTH // LANES):
            xc = p[:, c * LANES:(c + 1) * LANES]
            y = xc * lax.rsqrt(_half_lane_mean(xc * xc, lo) + EPS) * gain_ref[...]
            partner = jnp.where(first,
                                pltpu.roll(y, LANES - ROPE_DIM // 2, 1),
                                pltpu.roll(y, ROPE_DIM // 2, 1))
            y = y * cos + partner * sin
            out_ref[:, c * LANES:(c + 1) * LANES] = y.astype(out_ref.dtype)

    qk_section(0, qg_ref, q_out)
    qk_section(A_WIDTH, kg_ref, k_out)
    vt_out[...] = _dot_nt(wvt_ref[...], h).astype(vt_out.dtype)

    u = _gelu(_dot(h, w_ref[:, 3 * A_WIDTH:3 * A_WIDTH + B_WIDTH]))
    vg = _gelu(_dot(h, w_ref[:, 3 * A_WIDTH + B_WIDTH:3 * A_WIDTH + 2 * B_WIDTH]))

    row = lax.broadcasted_iota(jnp.int32, (GMLP_CHUNK, GMLP_CHUNK), 0)
    col = lax.broadcasted_iota(jnp.int32, (GMLP_CHUNK, GMLP_CHUNK), 1)
    tril = row >= col
    for c in range(B_WIDTH // LANES):
        sl = slice(c * LANES, (c + 1) * LANES)
        xc = vg[:, sl]
        xc = xc - _half_lane_mean(xc, lo)
        var = _half_lane_mean(xc * xc, lo)
        vn = xc * lax.rsqrt(var + EPS) * lng_ref[:, sl] + lnb_ref[:, sl]
        vn_lo = jnp.where(lo, vn, 0.0).astype(BF16)
        vn_hi = jnp.where(lo, 0.0, vn).astype(BF16)
        w_lo = jnp.where(tril, ws_ref[2 * c], 0.0).astype(BF16)
        w_hi = jnp.where(tril, ws_ref[2 * c + 1], 0.0).astype(BF16)
        bias = bs_ref[:, sl]
        for r in range(tm // GMLP_CHUNK):
            rs = slice(r * GMLP_CHUNK, (r + 1) * GMLP_CHUNK)
            mixed = _dot(w_lo, vn_lo[rs]) + _dot(w_hi, vn_hi[rs]) + bias
            g_out[rs, sl] = (u[rs, sl] * mixed).astype(g_out.dtype)


def _ev_proj(x, gn, w_in, q_gain, k_gain, cos_t, sin_t, ln_g, ln_b, w_s, b_s):
    t = x.shape[0]
    tm = ROW_TILE
    ev_in = w_in.shape[1]
    w_v_t = w_in[:, 2 * A_WIDTH:3 * A_WIDTH].T
    qg = jnp.tile(q_gain, LANES // HEAD_DIM)[None, :]
    kg = jnp.tile(k_gain, LANES // HEAD_DIM)[None, :]
    lng = ln_g.reshape(1, B_WIDTH)
    lnb = ln_b.reshape(1, B_WIDTH)
    bias = jnp.repeat(b_s.T, HEAD_DIM, axis=1)
    row_spec = lambda n: pl.BlockSpec((tm, n), lambda i: (i, 0))
    return pl.pallas_call(
        _ev_proj_kernel,
        grid=(t // tm,),
        in_specs=[row_spec(D_MODEL), _const_spec((1, D_MODEL)),
                  _const_spec((D_MODEL, ev_in)), _const_spec((A_WIDTH, D_MODEL)),
                  _const_spec((1, LANES)), _const_spec((1, LANES)),
                  row_spec(LANES), row_spec(LANES),
                  _const_spec((1, B_WIDTH)), _const_spec((1, B_WIDTH)),
                  _const_spec((B_GROUPS, GMLP_CHUNK, GMLP_CHUNK)),
                  _const_spec((GMLP_CHUNK, B_WIDTH))],
        out_specs=[row_spec(A_WIDTH), row_spec(A_WIDTH),
                   pl.BlockSpec((A_WIDTH, tm), lambda i: (0, i)), row_spec(B_WIDTH)],
        out_shape=[jax.ShapeDtypeStruct((t, A_WIDTH), BF16),
                   jax.ShapeDtypeStruct((t, A_WIDTH), BF16),
                   jax.ShapeDtypeStruct((A_WIDTH, t), BF16),
                   jax.ShapeDtypeStruct((t, B_WIDTH), BF16)],
        compiler_params=_params(1),
        name="ev_proj",
    )(x, gn[None, :], w_in, w_v_t, qg, kg, cos_t, sin_t, lng, lnb, w_s, bias)


def _moba_kernel(q_ref, k_ref, vt_ref, o_ref):
    seq = k_ref.shape[0]
    nb = seq // MOBA_BLOCK
    blk = MOBA_BLOCK
    scale = HEAD_DIM ** -0.5

    k_mean = jnp.mean(k_ref[...].astype(F32).reshape(nb, blk, LANES), axis=1)
    k_mean = jnp.concatenate([k_mean, jnp.zeros((16 - nb, LANES), F32)], axis=0).astype(BF16)

    lane = lax.broadcasted_iota(jnp.int32, (1, LANES), 1)
    lo = lane < HEAD_DIM
    key_i = lax.broadcasted_iota(jnp.int32, (blk, blk), 0)
    qry_i = lax.broadcasted_iota(jnp.int32, (blk, blk), 1)
    causal = key_i <= qry_i

    for i in range(nb):
        q_i = q_ref[i * blk:(i + 1) * blk, :]
        head_out = []
        for hd in range(LANES // HEAD_DIM):
            head_mask = lo if hd == 0 else jnp.logical_not(lo)
            qh = jnp.where(head_mask, q_i, jnp.zeros_like(q_i)) * scale
            s = [_dot_nt(k_ref[j * blk:(j + 1) * blk, :], qh) for j in range(i + 1)]
            s[i] = jnp.where(causal, s[i], NEG_INF)
            col_max = [jnp.max(sj, axis=0, keepdims=True) for sj in s]
            if i > MOBA_TOPK:
                gate = _dot_nt(k_mean, qh)
                g = [gate[j:j + 1, :] for j in range(i)]
                bias = []
                for j in range(i):
                    cnt = jnp.zeros_like(g[j])
                    for jp in range(i):
                        if jp == j:
                            continue
                        ahead = (g[jp] >= g[j]) if jp < j else (g[jp] > g[j])
                        cnt = cnt + jnp.where(ahead, 1.0, 0.0)
                    bias.append(jnp.where(cnt < MOBA_TOPK, 0.0, NEG_INF))
            else:
                bias = [None] * i
            m = col_max[i]
            for j in range(i):
                m = jnp.maximum(m, col_max[j] if bias[j] is None else col_max[j] + bias[j])
            l = jnp.zeros_like(m)
            acc = jnp.zeros((HEAD_DIM, blk), F32)
            for j in range(i + 1):
                shift = -m if (j == i or bias[j] is None) else bias[j] - m
                p = jnp.exp(s[j] + shift)
                l = l + jnp.sum(p, axis=0, keepdims=True)
                acc = acc + _dot(vt_ref[hd * HEAD_DIM:(hd + 1) * HEAD_DIM, j * blk:(j + 1) * blk],
                                 p.astype(BF16))
            head_out.append(acc / l)
        o_t = jnp.concatenate(head_out, axis=0)
        o_ref[i * blk:(i + 1) * blk, :] = o_t.T.astype(o_ref.dtype)


def _moba(q, k, v_t, batch, seq):
    n_pairs = A_WIDTH // LANES
    qk_spec = pl.BlockSpec((seq, LANES), lambda b, p: (b, p))
    vt_spec = pl.BlockSpec((LANES, seq), lambda b, p: (p, b))
    return pl.pallas_call(
        _moba_kernel,
        grid=(batch, n_pairs),
        in_specs=[qk_spec, qk_spec, vt_spec],
        out_specs=qk_spec,
        out_shape=jax.ShapeDtypeStruct(q.shape, BF16),
        compiler_params=_params(2),
        name="moba_attn",
    )(q, k, v_t)


def _od_mixer_kernel(x_ref, gn_ref, w_ref, wg1_ref, wg2_ref, bg_ref, og_ref, wo_ref,
                     o_ref, state_ref):
    @pl.when(pl.program_id(1) == 0)
    def _():
        state_ref[...] = jnp.zeros_like(state_ref)

    tm = x_ref.shape[0]
    c = GLA_CHUNK
    nc = tm // c
    x = x_ref[...]
    h = _rms_rows(x, gn_ref[...]).astype(BF16)
    q = _dot(h, w_ref[:, :GLA_QK_W]) * (GLA_DK ** -0.5)
    k = _dot(h, w_ref[:, GLA_QK_W:2 * GLA_QK_W])
    z = _dot(_dot(h, wg1_ref[...]).astype(BF16), wg2_ref[...]) + bg_ref[...]
    log_a = (jnp.minimum(z, 0.0) - jnp.log(1.0 + jnp.exp(-jnp.abs(z)))) * (1.0 / GLA_TAU)

    t_i = lax.broadcasted_iota(jnp.int32, (tm, tm), 0)
    s_i = lax.broadcasted_iota(jnp.int32, (tm, tm), 1)
    causal = jnp.logical_and(t_i // c == s_i // c, s_i <= t_i)
    b = jnp.dot(jnp.where(causal, 1.0, 0.0).astype(F32), log_a,
                preferred_element_type=F32, precision=lax.Precision.HIGHEST)
    b3 = b.reshape(nc, c, GLA_QK_W)
    b_last = b3[:, c - 1:c, :]
    q_in = (q * jnp.exp(b)).astype(BF16)
    k_in = (k * jnp.exp(-b)).astype(BF16)
    k_state = (k.reshape(nc, c, GLA_QK_W) * jnp.exp(b_last - b3)).reshape(tm, GLA_QK_W)
    k_state = k_state.astype(BF16)
    decay = jnp.exp(b_last)

    heads = []
    for hd in range(GLA_HEADS):
        ks = slice(hd * GLA_DK, (hd + 1) * GLA_DK)
        v0 = 2 * GLA_QK_W + hd * GLA_DV
        r0 = 2 * GLA_QK_W + GLA_V_W + hd * GLA_DV
        v_h = _dot(h, w_ref[:, v0:v0 + GLA_DV]).astype(BF16)
        r_h = _dot(h, w_ref[:, r0:r0 + GLA_DV])
        att = jnp.where(causal, _dot_nt(q_in[:, ks], k_in[:, ks]), 0.0).astype(BF16)
        o_intra = _dot(att, v_h)
        state = state_ref[hd]
        parts = []
        for n in range(nc):
            rs = slice(n * c, (n + 1) * c)
            parts.append(o_intra[rs] + _dot_nt(q_in[rs, ks], state.astype(BF16)))
            state = decay[n][:, ks] * state + _dot_tn(v_h[rs], k_state[rs, ks])
        state_ref[hd] = state
        o = jnp.concatenate(parts, axis=0)
        o = _rms_rows(o, og_ref[:, hd * GLA_DV:(hd + 1) * GLA_DV])
        heads.append((o * _silu(r_h)).astype(BF16))
    o_ref[...] = x + _dot(jnp.concatenate(heads, axis=-1), wo_ref[...])


def _od_mixer(x, gn, w_in, w_g1, w_g2, b_g, o_gain, w_out, batch, seq):
    tm = ROW_TILE
    steps = seq // tm
    wg1 = jnp.pad(w_g1, ((0, 0), (0, LANES - GLA_LOWRANK)))
    wg2 = jnp.pad(w_g2, ((0, LANES - GLA_LOWRANK), (0, 0)))
    row_spec = pl.BlockSpec((tm, D_MODEL), lambda b, i: (b * steps + i, 0))
    return pl.pallas_call(
        _od_mixer_kernel,
        grid=(batch, steps),
        in_specs=[row_spec, _const_spec((1, D_MODEL)), _const_spec(w_in.shape),
                  _const_spec(wg1.shape), _const_spec(wg2.shape),
                  _const_spec((1, GLA_QK_W)), _const_spec((1, GLA_V_W)),
                  _const_spec(w_out.shape)],
        out_specs=row_spec,
        out_shape=jax.ShapeDtypeStruct(x.shape, F32),
        scratch_shapes=[pltpu.VMEM((GLA_HEADS, GLA_DV, GLA_DK), F32)],
        compiler_params=_params(2),
        name="gla_mixer",
    )(x, gn[None, :], w_in, wg1, wg2, b_g[None, :], o_gain.reshape(1, GLA_V_W), w_out)


def _mem_kv_kernel(m_ref, gn_ref, w_ref, kg_ref, k_out, v_out):
    h = _rms_rows(m_ref[...], gn_ref[...]).astype(BF16)
    k = _dot(h, w_ref[:, :D_MODEL])
    for hh in range(X_HEADS):
        sl = slice(hh * X_HEAD_DIM, (hh + 1) * X_HEAD_DIM)
        k_out[:, sl] = _rms_rows(k[:, sl], kg_ref[...]).astype(k_out.dtype)
    v_out[...] = _dot(h, w_ref[:, D_MODEL:]).astype(v_out.dtype)


def _mem_kv(mem, gn, w_kv, k_gain):
    t = mem.shape[0]
    tm = ROW_TILE
    row_spec = pl.BlockSpec((tm, D_MODEL), lambda i: (i, 0))
    return pl.pallas_call(
        _mem_kv_kernel,
        grid=(t // tm,),
        in_specs=[row_spec, _const_spec((1, D_MODEL)), _const_spec(w_kv.shape),
                  _const_spec((1, X_HEAD_DIM))],
        out_specs=[row_spec, row_spec],
        out_shape=[jax.ShapeDtypeStruct((t, D_MODEL), BF16)] * 2,
        compiler_params=_params(1),
        name="mem_kv",
    )(mem, gn[None, :], w_kv, k_gain[None, :])


def _xattn_ffn_body(x, gq_ref, wq_ref, qg_ref, k_ref, v_ref, wo_ref,
                    gf_ref, wgu_ref, wd_ref, o_ref):
    h = _rms_rows(x, gq_ref[...]).astype(BF16)
    q = _dot(h, wq_ref[...])
    scale = X_HEAD_DIM ** -0.5
    heads = []
    for hh in range(X_HEADS):
        sl = slice(hh * X_HEAD_DIM, (hh + 1) * X_HEAD_DIM)
        qh = _rms_rows(q[:, sl], qg_ref[...]).astype(BF16)
        s = _dot_nt(qh, k_ref[:, sl]) * scale
        e = jnp.exp(s - jnp.max(s, axis=-1, keepdims=True))
        p = (e / jnp.sum(e, axis=-1, keepdims=True)).astype(BF16)
        heads.append(_dot(p, v_ref[:, sl]).astype(BF16))
    x = x + _dot(jnp.concatenate(heads, axis=-1), wo_ref[...])

    h = _rms_rows(x, gf_ref[...]).astype(BF16)
    acc = x
    for lo, hi in FFN_SPLITS:
        g = _dot(h, wgu_ref[:, lo:hi])
        u = _dot(h, wgu_ref[:, D_FF + lo:D_FF + hi])
        acc = acc + _dot((_silu(g) * u).astype(BF16), wd_ref[lo:hi, :])
    o_ref[...] = acc


def _xattn_ffn_kernel(x_ref, *rest):
    _xattn_ffn_body(x_ref[...], *rest)


def _proj_xattn_ffn_kernel(x_ref, a_ref, g_ref, wm_ref, *rest):
    n_a = a_ref.shape[1]
    x = (x_ref[...] + _dot(a_ref[...], wm_ref[:n_a, :])
         + _dot(g_ref[...], wm_ref[n_a:, :]))
    _xattn_ffn_body(x, *rest)


def _xattn_ffn(x, mixer, gq, w_q, q_gain, k_mem, v_mem, w_xo, gf, w_gu, w_down, batch, seq):
    tm = ROW_TILE
    steps = seq // tm
    row_spec = lambda n: pl.BlockSpec((tm, n), lambda b, i: (b * steps + i, 0))
    mem_spec = pl.BlockSpec((N_MEM, D_MODEL), lambda b, i: (b, 0))
    tail_specs = [_const_spec((1, D_MODEL)), _const_spec(w_q.shape),
                  _const_spec((1, X_HEAD_DIM)), mem_spec, mem_spec, _const_spec(w_xo.shape),
                  _const_spec((1, D_MODEL)), _const_spec(w_gu.shape), _const_spec(w_down.shape)]
    tail_args = (gq[None, :], w_q, q_gain[None, :], k_mem, v_mem, w_xo,
                 gf[None, :], w_gu, w_down)
    if mixer is None:
        body, head_specs, head_args = _xattn_ffn_kernel, [row_spec(D_MODEL)], (x,)
    else:
        a, g, w_m = mixer
        body = _proj_xattn_ffn_kernel
        head_specs = [row_spec(D_MODEL), row_spec(a.shape[1]), row_spec(g.shape[1]),
                      _const_spec(w_m.shape)]
        head_args = (x, a, g, w_m)
    return pl.pallas_call(
        body,
        grid=(batch, steps),
        in_specs=head_specs + tail_specs,
        out_specs=row_spec(D_MODEL),
        out_shape=jax.ShapeDtypeStruct(x.shape, F32),
        compiler_params=_params(2),
        name="xattn_ffn" if mixer is None else "proj_xattn_ffn",
    )(*head_args, *tail_args)


def kernel(x, mem, positions, norm_mix, norm_mem_q, norm_mem_kv, norm_ffn, ev_w_in, ev_q_gain, ev_k_gain, ev_w_s, ev_b_s, ev_ln_g, ev_ln_b, ev_w_out, od_w_in, od_w_g1, od_w_g2, od_b_g, od_o_gain, od_w_out, xa_w_q, xa_w_kv, xa_q_gain, xa_k_gain, xa_w_out, ffn_w_gu, ffn_w_down):
    batch, seq, d_model = x.shape
    assert d_model == D_MODEL and seq % ROW_TILE == 0 and seq % MOBA_BLOCK == 0
    assert mem.shape == (batch, N_MEM, D_MODEL)
    xf = x.reshape(batch * seq, D_MODEL)
    memf = mem.reshape(batch * N_MEM, D_MODEL)
    cos_t, sin_t = _rope_tables(positions)
    bf = lambda w: w.astype(BF16)
    for l in range(DEPTH):
        i = l // 2
        if l % 2 == 0:
            q, k, v_t, g = _ev_proj(xf, norm_mix[l], bf(ev_w_in[i]), ev_q_gain[i],
                                    ev_k_gain[i], cos_t, sin_t, ev_ln_g[i], ev_ln_b[i],
                                    ev_w_s[i], ev_b_s[i])
            mixer = (_moba(q, k, v_t, batch, seq), g, bf(ev_w_out[i]))
        else:
            xf = _od_mixer(xf, norm_mix[l], bf(od_w_in[i]), bf(od_w_g1[i]), bf(od_w_g2[i]),
                           od_b_g[i], od_o_gain[i], bf(od_w_out[i]), batch, seq)
            mixer = None
        k_mem, v_mem = _mem_kv(memf, norm_mem_kv[l], bf(xa_w_kv[l]), xa_k_gain[l])
        xf = _xattn_ffn(xf, mixer, norm_mem_q[l], bf(xa_w_q[l]), xa_q_gain[l], k_mem, v_mem,
                        bf(xa_w_out[l]), norm_ffn[l], bf(ffn_w_gu[l]), bf(ffn_w_down[l]),
                        batch, seq)
    return xf.reshape(batch, seq, D_MODEL)
```

```python
import functools
import math

import jax
import jax.numpy as jnp
from jax import lax
from jax.experimental import pallas as pl
from jax.experimental.pallas import tpu as pltpu

F32 = jnp.float32
BF16 = jnp.bfloat16

D_MODEL = 1024
DEPTH = 4
N_MEM = 256
EPS = 1e-6
NEG_INF = -1e30

HEAD_DIM = 64
ROPE_DIM = HEAD_DIM // 4
ROPE_THETA = 500000.0

A_HEADS = 8
A_WIDTH = A_HEADS * HEAD_DIM
MOBA_BLOCK = 256
MOBA_TOPK = 3

B_GROUPS = 8
B_WIDTH = B_GROUPS * HEAD_DIM
GMLP_CHUNK = 128

GLA_HEADS = 4
GLA_DK = (D_MODEL // 2) // GLA_HEADS
GLA_DV = D_MODEL // GLA_HEADS
GLA_QK_W = GLA_HEADS * GLA_DK
GLA_V_W = GLA_HEADS * GLA_DV
GLA_LOWRANK = 16
GLA_TAU = 16.0
GLA_CHUNK = 64
GLA_ATT_BLOCK = 256

X_HEADS = 4
X_HEAD_DIM = D_MODEL // X_HEADS

D_FF = -(-8 * D_MODEL // (3 * 256)) * 256

LANES = 128
VMEM_LIMIT_BYTES = 56 * 1024 * 1024

ROW_TILE = 512
FFN_SPLITS = ((0, 1536), (1536, D_FF))


def _params(n_axes):
    return pltpu.CompilerParams(
        dimension_semantics=("arbitrary",) * n_axes,
        vmem_limit_bytes=VMEM_LIMIT_BYTES)


def _const_spec(shape):
    nd = len(shape)
    return pl.BlockSpec(shape, lambda *_: (0,) * nd, pipeline_mode=pl.Buffered(1))


def _layer_spec(stacked, layer):
    nd = stacked.ndim - 1
    return pl.BlockSpec((None,) + stacked.shape[1:], lambda *_: (layer,) + (0,) * nd,
                        pipeline_mode=pl.Buffered(1))


def _dot(a, b):
    return jnp.dot(a, b, preferred_element_type=F32)


def _dot_nt(a, b):
    return lax.dot_general(a, b, (((1,), (1,)), ((), ())),
                           preferred_element_type=F32)


def _dot_tn(a, b):
    return lax.dot_general(a, b, (((0,), (0,)), ((), ())),
                           preferred_element_type=F32)


def _rms_rows(x, gain):
    ms = jnp.mean(x * x, axis=-1, keepdims=True)
    return x * lax.rsqrt(ms + EPS) * gain


def _half_lane_mean(x, lo):
    s_lo = jnp.sum(jnp.where(lo, x, 0.0), axis=-1, keepdims=True)
    s_hi = jnp.sum(jnp.where(lo, 0.0, x), axis=-1, keepdims=True)
    return jnp.where(lo, s_lo, s_hi) * (1.0 / HEAD_DIM)


def _gelu(x):
    return 0.5 * x * (1.0 + lax.erf(x * math.sqrt(0.5)))


def _silu(x):
    return x / (1.0 + jnp.exp(-x))


def _rope_table_kernel(pos_ref, invf_ref, sign_ref, cos_ref, sin_ref):
    ang = pos_ref[...] * invf_ref[...]
    cos_ref[...] = jnp.cos(ang)
    sin_ref[...] = jnp.sin(ang) * sign_ref[...]


def _rope_tables(positions):
    t = positions.size
    rows = 2048
    half = ROPE_DIM // 2
    inv_freq = ROPE_THETA ** (-jnp.arange(half, dtype=F32) * 2.0 / ROPE_DIM)
    d = jnp.arange(LANES) % HEAD_DIM
    invf = jnp.where(d < ROPE_DIM, inv_freq[d % half], 0.0).astype(F32)[None, :]
    sign = jnp.where(d < half, -1.0, jnp.where(d < ROPE_DIM, 1.0, 0.0)).astype(F32)[None, :]
    pos = positions.astype(F32).reshape(t, 1)
    return pl.pallas_call(
        _rope_table_kernel,
        grid=(t // rows,),
        in_specs=[pl.BlockSpec((rows, 1), lambda i: (i, 0)),
                  _const_spec((1, LANES)), _const_spec((1, LANES))],
        out_specs=[pl.BlockSpec((rows, LANES), lambda i: (i, 0))] * 2,
        out_shape=[jax.ShapeDtypeStruct((t, LANES), F32)] * 2,
        compiler_params=_params(1),
        name="rope_tables",
    )(pos, invf, sign)


def _ev_proj_kernel(x_ref, gn_ref, w_ref, wvt_ref, qg_ref, kg_ref, cos_ref, sin_ref,
                    lng_ref, lnb_ref, ws_ref, bs_ref,
                    q_out, k_out, vt_out, g_out):
    tm = x_ref.shape[0]
    h = _rms_rows(x_ref[...], gn_ref[...]).astype(BF16)
    lane = lax.broadcasted_iota(jnp.int32, (1, LANES), 1)
    lo = lane < HEAD_DIM
    first = (lane % HEAD_DIM) < (ROPE_DIM // 2)
    cos = cos_ref[...]
    sin = sin_ref[...]

    def qk_section(col0, gain_ref, out_ref):
        p = _dot(h, w_ref[:, col0:col0 + A_WIDTH])
        for c in range(A_WIDTH // LANES):
            xc = p[:, c * LANES:(c + 1) * LANES]
            y = xc * lax.rsqrt(_half_lane_mean(xc * xc, lo) + EPS) * gain_ref[...]
            partner = jnp.where(first,
                                pltpu.roll(y, LANES - ROPE_DIM // 2, 1),
                                pltpu.roll(y, ROPE_DIM // 2, 1))
            y = y * cos + partner * sin
            out_ref[:, c * LANES:(c + 1) * LANES] = y.astype(out_ref.dtype)

    qk_section(0, qg_ref, q_out)
    qk_section(A_WIDTH, kg_ref, k_out)
    vt_out[...] = _dot_nt(wvt_ref[...], h).astype(vt_out.dtype)

    u = _gelu(_dot(h, w_ref[:, 3 * A_WIDTH:3 * A_WIDTH + B_WIDTH]))
    vg = _gelu(_dot(h, w_ref[:, 3 * A_WIDTH + B_WIDTH:3 * A_WIDTH + 2 * B_WIDTH]))

    row = lax.broadcasted_iota(jnp.int32, (GMLP_CHUNK, GMLP_CHUNK), 0)
    col = lax.broadcasted_iota(jnp.int32, (GMLP_CHUNK, GMLP_CHUNK), 1)
    tril = row >= col
    for c in range(B_WIDTH // LANES):
        sl = slice(c * LANES, (c + 1) * LANES)
        xc = vg[:, sl]
        xc = xc - _half_lane_mean(xc, lo)
        var = _half_lane_mean(xc * xc, lo)
        vn = xc * lax.rsqrt(var + EPS) * lng_ref[:, sl] + lnb_ref[:, sl]
        vn_lo = jnp.where(lo, vn, 0.0).astype(BF16)
        vn_hi = jnp.where(lo, 0.0, vn).astype(BF16)
        w_lo = jnp.where(tril, ws_ref[2 * c], 0.0).astype(BF16)
        w_hi = jnp.where(tril, ws_ref[2 * c + 1], 0.0).astype(BF16)
        bias = bs_ref[:, sl]
        for r in range(tm // GMLP_CHUNK):
            rs = slice(r * GMLP_CHUNK, (r + 1) * GMLP_CHUNK)
            mixed = _dot(w_lo, vn_lo[rs]) + _dot(w_hi, vn_hi[rs]) + bias
            g_out[rs, sl] = (u[rs, sl] * mixed).astype(g_out.dtype)


def _ev_params(norm_mix, w_in, q_gain, k_gain, ln_g, ln_b, w_s, b_s):
    n_ev = w_in.shape[0]
    gn = norm_mix[0::2].reshape(n_ev, 1, D_MODEL)
    w_v_t = jnp.swapaxes(w_in[:, :, 2 * A_WIDTH:3 * A_WIDTH], 1, 2).astype(BF16)
    reps = LANES // HEAD_DIM
    qg = jnp.tile(q_gain, (1, reps)).reshape(n_ev, 1, LANES)
    kg = jnp.tile(k_gain, (1, reps)).reshape(n_ev, 1, LANES)
    lng = ln_g.reshape(n_ev, 1, B_WIDTH)
    lnb = ln_b.reshape(n_ev, 1, B_WIDTH)
    bias = jnp.repeat(jnp.swapaxes(b_s, 1, 2), HEAD_DIM, axis=2)
    return dict(gn=gn, w_in=w_in.astype(BF16), w_v_t=w_v_t, qg=qg, kg=kg,
                lng=lng, lnb=lnb, w_s=w_s, bias=bias)


def _ev_proj(x, p, layer, cos_t, sin_t):
    t = x.shape[0]
    tm = ROW_TILE
    row_spec = lambda n: pl.BlockSpec((tm, n), lambda i: (i, 0))
    ls = lambda name: _layer_spec(p[name], layer)
    return pl.pallas_call(
        _ev_proj_kernel,
        grid=(t // tm,),
        in_specs=[row_spec(D_MODEL), ls("gn"), ls("w_in"), ls("w_v_t"), ls("qg"), ls("kg"),
                  row_spec(LANES), row_spec(LANES),
                  ls("lng"), ls("lnb"), ls("w_s"), ls("bias")],
        out_specs=[row_spec(A_WIDTH), row_spec(A_WIDTH),
                   pl.BlockSpec((A_WIDTH, tm), lambda i: (0, i)), row_spec(B_WIDTH)],
        out_shape=[jax.ShapeDtypeStruct((t, A_WIDTH), BF16),
                   jax.ShapeDtypeStruct((t, A_WIDTH), BF16),
                   jax.ShapeDtypeStruct((A_WIDTH, t), BF16),
                   jax.ShapeDtypeStruct((t, B_WIDTH), BF16)],
        compiler_params=_params(1),
        name="ev_proj",
    )(x, p["gn"], p["w_in"], p["w_v_t"], p["qg"], p["kg"], cos_t, sin_t,
      p["lng"], p["lnb"], p["w_s"], p["bias"])


def _moba_kernel(q_ref, k_ref, vt_ref, o_ref):
    seq = k_ref.shape[0]
    nb = seq // MOBA_BLOCK
    blk = MOBA_BLOCK
    scale = HEAD_DIM ** -0.5

    k_mean = jnp.mean(k_ref[...].astype(F32).reshape(nb, blk, LANES), axis=1)
    k_mean = jnp.concatenate([k_mean, jnp.zeros((16 - nb, LANES), F32)], axis=0).astype(BF16)

    lane = lax.broadcasted_iota(jnp.int32, (1, LANES), 1)
    lo = lane < HEAD_DIM
    key_i = lax.broadcasted_iota(jnp.int32, (blk, blk), 0)
    qry_i = lax.broadcasted_iota(jnp.int32, (blk, blk), 1)
    causal = key_i <= qry_i

    for i in range(nb):
        q_i = q_ref[i * blk:(i + 1) * blk, :]
        head_out = []
        for hd in range(LANES // HEAD_DIM):
            head_mask = lo if hd == 0 else jnp.logical_not(lo)
            qh = jnp.where(head_mask, q_i, jnp.zeros_like(q_i)) * scale
            s = [_dot_nt(k_ref[j * blk:(j + 1) * blk, :], qh) for j in range(i + 1)]
            s[i] = jnp.where(causal, s[i], NEG_INF)
            col_max = [jnp.max(sj, axis=0, keepdims=True) for sj in s]
            if i > MOBA_TOPK:
                gate = _dot_nt(k_mean, qh)
                g = [gate[j:j + 1, :] for j in range(i)]
                bias = []
                for j in range(i):
                    cnt = jnp.zeros_like(g[j])
                    for jp in range(i):
                        if jp == j:
                            continue
                        ahead = (g[jp] >= g[j]) if jp < j else (g[jp] > g[j])
                        cnt = cnt + jnp.where(ahead, 1.0, 0.0)
                    bias.append(jnp.where(cnt < MOBA_TOPK, 0.0, NEG_INF))
            else:
                bias = [None] * i
            m = col_max[i]
            for j in range(i):
                m = jnp.maximum(m, col_max[j] if bias[j] is None else col_max[j] + bias[j])
            l = jnp.zeros_like(m)
            p_blocks = []
            for j in range(i + 1):
                shift = -m if (j == i or bias[j] is None) else bias[j] - m
                p = jnp.exp(s[j] + shift)
                l = l + jnp.sum(p, axis=0, keepdims=True)
                p_blocks.append(p.astype(BF16))
            p_all = p_blocks[0] if i == 0 else jnp.concatenate(p_blocks, axis=0)
            acc = _dot(vt_ref[hd * HEAD_DIM:(hd + 1) * HEAD_DIM, 0:(i + 1) * blk], p_all)
            head_out.append(acc / l)
        o_t = jnp.concatenate(head_out, axis=0)
        o_ref[i * blk:(i + 1) * blk, :] = o_t.T.astype(o_ref.dtype)


def _moba(q, k, v_t, batch, seq):
    n_pairs = A_WIDTH // LANES
    qk_spec = pl.BlockSpec((seq, LANES), lambda b, p: (b, p))
    vt_spec = pl.BlockSpec((LANES, seq), lambda b, p: (p, b))
    return pl.pallas_call(
        _moba_kernel,
        grid=(batch, n_pairs),
        in_specs=[qk_spec, qk_spec, vt_spec],
        out_specs=qk_spec,
        out_shape=jax.ShapeDtypeStruct(q.shape, BF16),
        compiler_params=_params(2),
        name="moba_attn",
    )(q, k, v_t)


def _od_mixer_kernel(x_ref, gn_ref, w_ref, wg1_ref, wg2_ref, bg_ref, og_ref, wo_ref,
                     o_ref, state_ref):
    @pl.when(pl.program_id(1) == 0)
    def _():
        state_ref[...] = jnp.zeros_like(state_ref)

    tm = x_ref.shape[0]
    c = GLA_CHUNK
    nc = tm // c
    x = x_ref[...]
    h = _rms_rows(x, gn_ref[...]).astype(BF16)
    q = _dot(h, w_ref[:, :GLA_QK_W]) * (GLA_DK ** -0.5)
    k = _dot(h, w_ref[:, GLA_QK_W:2 * GLA_QK_W])
    z = _dot(_dot(h, wg1_ref[...]).astype(BF16), wg2_ref[...]) + bg_ref[...]
    log_a = (jnp.minimum(z, 0.0) - jnp.log(1.0 + jnp.exp(-jnp.abs(z)))) * (1.0 / GLA_TAU)

    in_chunk = lax.broadcasted_iota(jnp.int32, (tm, GLA_QK_W), 0) % c
    b = log_a
    step = 1
    while step < c:
        b = b + jnp.where(in_chunk >= step, pltpu.roll(b, step, 0), 0.0)
        step *= 2
    ab = GLA_ATT_BLOCK
    t_i = lax.broadcasted_iota(jnp.int32, (ab, ab), 0)
    s_i = lax.broadcasted_iota(jnp.int32, (ab, ab), 1)
    causal = jnp.logical_and(t_i // c == s_i // c, s_i <= t_i)
    b3 = b.reshape(nc, c, GLA_QK_W)
    b_last = b3[:, c - 1:c, :]
    q_in = (q * jnp.exp(b)).astype(BF16)
    k_in = (k * jnp.exp(-b)).astype(BF16)
    k_state = (k.reshape(nc, c, GLA_QK_W) * jnp.exp(b_last - b3)).reshape(tm, GLA_QK_W)
    k_state = k_state.astype(BF16)
    decay = jnp.exp(b_last)

    heads = []
    for hd in range(GLA_HEADS):
        ks = slice(hd * GLA_DK, (hd + 1) * GLA_DK)
        v0 = 2 * GLA_QK_W + hd * GLA_DV
        r0 = 2 * GLA_QK_W + GLA_V_W + hd * GLA_DV
        v_h = _dot(h, w_ref[:, v0:v0 + GLA_DV]).astype(BF16)
        r_h = _dot(h, w_ref[:, r0:r0 + GLA_DV])
        o_intra = []
        for a0 in range(0, tm, ab):
            rows = slice(a0, a0 + ab)
            att = jnp.where(causal, _dot_nt(q_in[rows, ks], k_in[rows, ks]), 0.0)
            o_intra.append(_dot(att.astype(BF16), v_h[rows]))
        o_intra = jnp.concatenate(o_intra, axis=0)
        state = state_ref[hd]
        parts = []
        for n in range(nc):
            rs = slice(n * c, (n + 1) * c)
            parts.append(o_intra[rs] + _dot_nt(q_in[rs, ks], state.astype(BF16)))
            state = decay[n][:, ks] * state + _dot_tn(v_h[rs], k_state[rs, ks])
        state_ref[hd] = state
        o = jnp.concatenate(parts, axis=0)
        o = _rms_rows(o, og_ref[:, hd * GLA_DV:(hd + 1) * GLA_DV])
        heads.append((o * _silu(r_h)).astype(BF16))
    o_ref[...] = x + _dot(jnp.concatenate(heads, axis=-1), wo_ref[...])


def _od_params(norm_mix, w_in, w_g1, w_g2, b_g, o_gain, w_out):
    n_od = w_in.shape[0]
    wg1 = jnp.pad(w_g1, ((0, 0), (0, 0), (0, LANES - GLA_LOWRANK))).astype(BF16)
    wg2 = jnp.pad(w_g2, ((0, 0), (0, LANES - GLA_LOWRANK), (0, 0))).astype(BF16)
    return (norm_mix[1::2].reshape(n_od, 1, D_MODEL), w_in.astype(BF16), wg1, wg2,
            b_g.reshape(n_od, 1, GLA_QK_W), o_gain.reshape(n_od, 1, GLA_V_W),
            w_out.astype(BF16))


def _od_mixer(x, params, layer, batch, seq):
    tm = ROW_TILE
    steps = seq // tm
    row_spec = pl.BlockSpec((tm, D_MODEL), lambda b, i: (b * steps + i, 0))
    return pl.pallas_call(
        _od_mixer_kernel,
        grid=(batch, steps),
        in_specs=[row_spec] + [_layer_spec(a, layer) for a in params],
        out_specs=row_spec,
        out_shape=jax.ShapeDtypeStruct(x.shape, F32),
        scratch_shapes=[pltpu.VMEM((GLA_HEADS, GLA_DV, GLA_DK), F32)],
        compiler_params=_params(2),
        name="gla_mixer",
    )(x, *params)


def _mem_kv_kernel(m_ref, gn_ref, w_ref, kg_ref, k_out, v_out):
    h = _rms_rows(m_ref[...], gn_ref[...]).astype(BF16)
    k = _dot(h, w_ref[:, :D_MODEL])
    for hh in range(X_HEADS):
        sl = slice(hh * X_HEAD_DIM, (hh + 1) * X_HEAD_DIM)
        k_out[:, sl] = _rms_rows(k[:, sl], kg_ref[...]).astype(k_out.dtype)
    v_out[...] = _dot(h, w_ref[:, D_MODEL:]).astype(v_out.dtype)


def _mem_kv(mem, gn, w_kv, k_gain):
    t = mem.shape[0]
    depth = w_kv.shape[0]
    tm = ROW_TILE
    per_layer = lambda a: pl.BlockSpec((None,) + a.shape[1:], lambda l, i: (l, 0, 0))
    out_spec = pl.BlockSpec((None, tm, D_MODEL), lambda l, i: (l, i, 0))
    return pl.pallas_call(
        _mem_kv_kernel,
        grid=(depth, t // tm),
        in_specs=[pl.BlockSpec((tm, D_MODEL), lambda l, i: (i, 0)),
                  per_layer(gn), per_layer(w_kv), per_layer(k_gain)],
        out_specs=[out_spec, out_spec],
        out_shape=[jax.ShapeDtypeStruct((depth, t, D_MODEL), BF16)] * 2,
        compiler_params=_params(2),
        name="mem_kv",
    )(mem, gn, w_kv, k_gain)


def _xattn_ffn_body(x, gq_ref, wq_ref, qg_ref, k_ref, v_ref, wo_ref,
                    gf_ref, wgu_ref, wd_ref, o_ref):
    h = _rms_rows(x, gq_ref[...]).astype(BF16)
    q = _dot(h, wq_ref[...])
    scale = X_HEAD_DIM ** -0.5
    heads = []
    for hh in range(X_HEADS):
        sl = slice(hh * X_HEAD_DIM, (hh + 1) * X_HEAD_DIM)
        qh = _rms_rows(q[:, sl], qg_ref[...]).astype(BF16)
        s = _dot_nt(qh, k_ref[:, sl]) * scale
        e = jnp.exp(s - jnp.max(s, axis=-1, keepdims=True))
        p = (e / jnp.sum(e, axis=-1, keepdims=True)).astype(BF16)
        heads.append(_dot(p, v_ref[:, sl]).astype(BF16))
    x = x + _dot(jnp.concatenate(heads, axis=-1), wo_ref[...])

    h = _rms_rows(x, gf_ref[...]).astype(BF16)
    acc = x
    for lo, hi in FFN_SPLITS:
        g = _dot(h, wgu_ref[:, lo:hi])
        u = _dot(h, wgu_ref[:, D_FF + lo:D_FF + hi])
        acc = acc + _dot((_silu(g) * u).astype(BF16), wd_ref[lo:hi, :])
    o_ref[...] = acc


def _xattn_ffn_kernel(x_ref, *rest):
    _xattn_ffn_body(x_ref[...], *rest)


def _proj_xattn_ffn_kernel(x_ref, a_ref, g_ref, wm_ref, *rest):
    n_a = a_ref.shape[1]
    x = (x_ref[...] + _dot(a_ref[...], wm_ref[:n_a, :])
         + _dot(g_ref[...], wm_ref[n_a:, :]))
    _xattn_ffn_body(x, *rest)


def _xattn_ffn(x, mixer, p, layer, k_mem, v_mem, batch, seq):
    tm = ROW_TILE
    steps = seq // tm
    row_spec = lambda n: pl.BlockSpec((tm, n), lambda b, i: (b * steps + i, 0))
    mem_spec = pl.BlockSpec((None, N_MEM, D_MODEL), lambda b, i: (layer, b, 0))
    ls = lambda name: _layer_spec(p[name], layer)
    tail_specs = [ls("gq"), ls("w_q"), ls("q_gain"), mem_spec, mem_spec, ls("w_xo"),
                  ls("gf"), ls("w_gu"), ls("w_down")]
    tail_args = (p["gq"], p["w_q"], p["q_gain"], k_mem, v_mem, p["w_xo"],
                 p["gf"], p["w_gu"], p["w_down"])
    if mixer is None:
        body, head_specs, head_args = _xattn_ffn_kernel, [row_spec(D_MODEL)], (x,)
    else:
        a, g, w_m, m_layer = mixer
        body = _proj_xattn_ffn_kernel
        head_specs = [row_spec(D_MODEL), row_spec(a.shape[1]), row_spec(g.shape[1]),
                      _layer_spec(w_m, m_layer)]
        head_args = (x, a, g, w_m)
    return pl.pallas_call(
        body,
        grid=(batch, steps),
        in_specs=head_specs + tail_specs,
        out_specs=row_spec(D_MODEL),
        out_shape=jax.ShapeDtypeStruct(x.shape, F32),
        compiler_params=_params(2),
        name="xattn_ffn" if mixer is None else "proj_xattn_ffn",
    )(*head_args, *tail_args)


def kernel(x, mem, positions, norm_mix, norm_mem_q, norm_mem_kv, norm_ffn, ev_w_in, ev_q_gain, ev_k_gain, ev_w_s, ev_b_s, ev_ln_g, ev_ln_b, ev_w_out, od_w_in, od_w_g1, od_w_g2, od_b_g, od_o_gain, od_w_out, xa_w_q, xa_w_kv, xa_q_gain, xa_k_gain, xa_w_out, ffn_w_gu, ffn_w_down):
    batch, seq, d_model = x.shape
    assert d_model == D_MODEL and seq % ROW_TILE == 0 and seq % MOBA_BLOCK == 0
    assert mem.shape == (batch, N_MEM, D_MODEL)
    xf = x.reshape(batch * seq, D_MODEL)
    memf = mem.reshape(batch * N_MEM, D_MODEL)
    cos_t, sin_t = _rope_tables(positions)
    vec = lambda a: a.reshape(a.shape[0], 1, a.shape[1])
    ev = _ev_params(norm_mix, ev_w_in, ev_q_gain, ev_k_gain, ev_ln_g, ev_ln_b, ev_w_s, ev_b_s)
    ev_w_out_bf = ev_w_out.astype(BF16)
    od = _od_params(norm_mix, od_w_in, od_w_g1, od_w_g2, od_b_g, od_o_gain, od_w_out)
    post = dict(gq=vec(norm_mem_q), w_q=xa_w_q.astype(BF16), q_gain=vec(xa_q_gain),
                w_xo=xa_w_out.astype(BF16), gf=vec(norm_ffn),
                w_gu=ffn_w_gu.astype(BF16), w_down=ffn_w_down.astype(BF16))
    k_mem, v_mem = _mem_kv(memf, vec(norm_mem_kv), xa_w_kv.astype(BF16), vec(xa_k_gain))
    for l in range(DEPTH):
        i = l // 2
        if l % 2 == 0:
            q, k, v_t, g = _ev_proj(xf, ev, i, cos_t, sin_t)
            mixer = (_moba(q, k, v_t, batch, seq), g, ev_w_out_bf, i)
        else:
            xf = _od_mixer(xf, od, i, batch, seq)
            mixer = None
        xf = _xattn_ffn(xf, mixer, post, l, k_mem, v_mem, batch, seq)
    return xf.reshape(batch, seq, D_MODEL)
```

```python
import functools
import math

import jax
import jax.numpy as jnp
from jax import lax
from jax.experimental import pallas as pl
from jax.experimental.pallas import tpu as pltpu

F32 = jnp.float32
BF16 = jnp.bfloat16

D_MODEL = 1024
DEPTH = 4
N_MEM = 256
EPS = 1e-6
NEG_INF = -1e30

HEAD_DIM = 64
ROPE_DIM = HEAD_DIM // 4
ROPE_THETA = 500000.0

A_HEADS = 8
A_WIDTH = A_HEADS * HEAD_DIM
MOBA_BLOCK = 256
MOBA_TOPK = 3
MOBA_LOOKAHEAD = 3

B_GROUPS = 8
B_WIDTH = B_GROUPS * HEAD_DIM
GMLP_CHUNK = 128

GLA_HEADS = 4
GLA_DK = (D_MODEL // 2) // GLA_HEADS
GLA_DV = D_MODEL // GLA_HEADS
GLA_QK_W = GLA_HEADS * GLA_DK
GLA_V_W = GLA_HEADS * GLA_DV
GLA_LOWRANK = 16
GLA_TAU = 16.0
GLA_CHUNK = 64
GLA_ATT_BLOCK = 256

X_HEADS = 4
X_HEAD_DIM = D_MODEL // X_HEADS

D_FF = -(-8 * D_MODEL // (3 * 256)) * 256

LANES = 128
VMEM_LIMIT_BYTES = 56 * 1024 * 1024

ROW_TILE = 512
FFN_SPLITS = ((0, 1536), (1536, D_FF))


def _params(n_axes):
    return pltpu.CompilerParams(
        dimension_semantics=("arbitrary",) * n_axes,
        vmem_limit_bytes=VMEM_LIMIT_BYTES)


def _const_spec(shape):
    nd = len(shape)
    return pl.BlockSpec(shape, lambda *_: (0,) * nd, pipeline_mode=pl.Buffered(1))


def _layer_spec(stacked, layer):
    nd = stacked.ndim - 1
    return pl.BlockSpec((None,) + stacked.shape[1:], lambda *_: (layer,) + (0,) * nd,
                        pipeline_mode=pl.Buffered(1))


def _dot(a, b):
    return jnp.dot(a, b, preferred_element_type=F32)


def _dot_nt(a, b):
    return lax.dot_general(a, b, (((1,), (1,)), ((), ())),
                           preferred_element_type=F32)


def _dot_tn(a, b):
    return lax.dot_general(a, b, (((0,), (0,)), ((), ())),
                           preferred_element_type=F32)


def _rms_rows(x, gain):
    ms = jnp.mean(x * x, axis=-1, keepdims=True)
    return x * lax.rsqrt(ms + EPS) * gain


def _half_lane_mean(x, lo):
    s_lo = jnp.sum(jnp.where(lo, x, 0.0), axis=-1, keepdims=True)
    s_hi = jnp.sum(jnp.where(lo, 0.0, x), axis=-1, keepdims=True)
    return jnp.where(lo, s_lo, s_hi) * (1.0 / HEAD_DIM)


def _gelu(x):
    return 0.5 * x * (1.0 + lax.erf(x * math.sqrt(0.5)))


def _silu(x):
    return x / (1.0 + jnp.exp(-x))


def _rope_table_kernel(pos_ref, invf_ref, sign_ref, cos_ref, sin_ref):
    ang = pos_ref[...] * invf_ref[...]
    cos_ref[...] = jnp.cos(ang)
    sin_ref[...] = jnp.sin(ang) * sign_ref[...]


def _rope_tables(positions):
    t = positions.size
    rows = 2048
    half = ROPE_DIM // 2
    inv_freq = ROPE_THETA ** (-jnp.arange(half, dtype=F32) * 2.0 / ROPE_DIM)
    d = jnp.arange(LANES) % HEAD_DIM
    invf = jnp.where(d < ROPE_DIM, inv_freq[d % half], 0.0).astype(F32)[None, :]
    sign = jnp.where(d < half, -1.0, jnp.where(d < ROPE_DIM, 1.0, 0.0)).astype(F32)[None, :]
    pos = positions.astype(F32).reshape(t, 1)
    return pl.pallas_call(
        _rope_table_kernel,
        grid=(t // rows,),
        in_specs=[pl.BlockSpec((rows, 1), lambda i: (i, 0)),
                  _const_spec((1, LANES)), _const_spec((1, LANES))],
        out_specs=[pl.BlockSpec((rows, LANES), lambda i: (i, 0))] * 2,
        out_shape=[jax.ShapeDtypeStruct((t, LANES), F32)] * 2,
        compiler_params=_params(1),
        name="rope_tables",
    )(pos, invf, sign)


def _ev_proj_kernel(x_ref, gn_ref, w_ref, wvt_ref, qg_ref, kg_ref, cos_ref, sin_ref,
                    lng_ref, lnb_ref, ws_ref, bs_ref,
                    q_out, k_out, vt_out, g_out):
    tm = x_ref.shape[0]
    h = _rms_rows(x_ref[...], gn_ref[...]).astype(BF16)
    lane = lax.broadcasted_iota(jnp.int32, (1, LANES), 1)
    lo = lane < HEAD_DIM
    first = (lane % HEAD_DIM) < (ROPE_DIM // 2)
    cos = cos_ref[...]
    sin = sin_ref[...]

    def qk_section(col0, gain_ref, out_ref):
        p = _dot(h, w_ref[:, col0:col0 + A_WIDTH])
        for c in range(A_WIDTH // LANES):
            xc = p[:, c * LANES:(c + 1) * LANES]
            y = xc * lax.rsqrt(_half_lane_mean(xc * xc, lo) + EPS) * gain_ref[...]
            partner = jnp.where(first,
                                pltpu.roll(y, LANES - ROPE_DIM // 2, 1),
                                pltpu.roll(y, ROPE_DIM // 2, 1))
            y = y * cos + partner * sin
            out_ref[:, c * LANES:(c + 1) * LANES] = y.astype(out_ref.dtype)

    qk_section(0, qg_ref, q_out)
    qk_section(A_WIDTH, kg_ref, k_out)
    vt_out[...] = _dot_nt(wvt_ref[...], h).astype(vt_out.dtype)

    u = _gelu(_dot(h, w_ref[:, 3 * A_WIDTH:3 * A_WIDTH + B_WIDTH]))
    vg = _gelu(_dot(h, w_ref[:, 3 * A_WIDTH + B_WIDTH:3 * A_WIDTH + 2 * B_WIDTH]))

    row = lax.broadcasted_iota(jnp.int32, (GMLP_CHUNK, GMLP_CHUNK), 0)
    col = lax.broadcasted_iota(jnp.int32, (GMLP_CHUNK, GMLP_CHUNK), 1)
    tril = row >= col
    for c in range(B_WIDTH // LANES):
        sl = slice(c * LANES, (c + 1) * LANES)
        xc = vg[:, sl]
        xc = xc - _half_lane_mean(xc, lo)
        var = _half_lane_mean(xc * xc, lo)
        vn = xc * lax.rsqrt(var + EPS) * lng_ref[:, sl] + lnb_ref[:, sl]
        vn_lo = jnp.where(lo, vn, 0.0).astype(BF16)
        vn_hi = jnp.where(lo, 0.0, vn).astype(BF16)
        w_lo = jnp.where(tril, ws_ref[2 * c], 0.0).astype(BF16)
        w_hi = jnp.where(tril, ws_ref[2 * c + 1], 0.0).astype(BF16)
        bias = bs_ref[:, sl]
        for r in range(tm // GMLP_CHUNK):
            rs = slice(r * GMLP_CHUNK, (r + 1) * GMLP_CHUNK)
            mixed = _dot(w_lo, vn_lo[rs]) + _dot(w_hi, vn_hi[rs]) + bias
            g_out[rs, sl] = (u[rs, sl] * mixed).astype(g_out.dtype)


def _ev_params(norm_mix, w_in, q_gain, k_gain, ln_g, ln_b, w_s, b_s):
    n_ev = w_in.shape[0]
    gn = norm_mix[0::2].reshape(n_ev, 1, D_MODEL)
    w_v_t = jnp.swapaxes(w_in[:, :, 2 * A_WIDTH:3 * A_WIDTH], 1, 2).astype(BF16)
    reps = LANES // HEAD_DIM
    qg = jnp.tile(q_gain, (1, reps)).reshape(n_ev, 1, LANES)
    kg = jnp.tile(k_gain, (1, reps)).reshape(n_ev, 1, LANES)
    lng = ln_g.reshape(n_ev, 1, B_WIDTH)
    lnb = ln_b.reshape(n_ev, 1, B_WIDTH)
    bias = jnp.repeat(jnp.swapaxes(b_s, 1, 2), HEAD_DIM, axis=2)
    return dict(gn=gn, w_in=w_in.astype(BF16), w_v_t=w_v_t, qg=qg, kg=kg,
                lng=lng, lnb=lnb, w_s=w_s, bias=bias)


def _ev_proj(x, p, layer, cos_t, sin_t):
    t = x.shape[0]
    tm = ROW_TILE
    row_spec = lambda n: pl.BlockSpec((tm, n), lambda i: (i, 0))
    ls = lambda name: _layer_spec(p[name], layer)
    return pl.pallas_call(
        _ev_proj_kernel,
        grid=(t // tm,),
        in_specs=[row_spec(D_MODEL), ls("gn"), ls("w_in"), ls("w_v_t"), ls("qg"), ls("kg"),
                  row_spec(LANES), row_spec(LANES),
                  ls("lng"), ls("lnb"), ls("w_s"), ls("bias")],
        out_specs=[row_spec(A_WIDTH), row_spec(A_WIDTH),
                   pl.BlockSpec((A_WIDTH, tm), lambda i: (0, i)), row_spec(B_WIDTH)],
        out_shape=[jax.ShapeDtypeStruct((t, A_WIDTH), BF16),
                   jax.ShapeDtypeStruct((t, A_WIDTH), BF16),
                   jax.ShapeDtypeStruct((A_WIDTH, t), BF16),
                   jax.ShapeDtypeStruct((t, B_WIDTH), BF16)],
        compiler_params=_params(1),
        name="ev_proj",
    )(x, p["gn"], p["w_in"], p["w_v_t"], p["qg"], p["kg"], cos_t, sin_t,
      p["lng"], p["lnb"], p["w_s"], p["bias"])


def _moba_kernel(q_ref, k_ref, vt_ref, o_ref, s_ref):
    seq = k_ref.shape[0]
    nb = seq // MOBA_BLOCK
    blk = MOBA_BLOCK
    scale = HEAD_DIM ** -0.5

    k_mean = jnp.mean(k_ref[...].astype(F32).reshape(nb, blk, LANES), axis=1)
    k_mean = jnp.concatenate([k_mean, jnp.zeros((16 - nb, LANES), F32)], axis=0).astype(BF16)

    lane = lax.broadcasted_iota(jnp.int32, (1, LANES), 1)
    lo = lane < HEAD_DIM
    key_i = lax.broadcasted_iota(jnp.int32, (blk, blk), 0)
    qry_i = lax.broadcasted_iota(jnp.int32, (blk, blk), 1)
    causal = key_i <= qry_i

    def scores(i, hd, slot):
        q_i = q_ref[i * blk:(i + 1) * blk, :]
        head_mask = lo if hd == 0 else jnp.logical_not(lo)
        qh = jnp.where(head_mask, q_i, jnp.zeros_like(q_i)) * scale
        col_max = []
        for j in range(i + 1):
            sj = _dot_nt(k_ref[j * blk:(j + 1) * blk, :], qh)
            if j == i:
                sj = jnp.where(causal, sj, NEG_INF)
            s_ref[slot, j * blk:(j + 1) * blk, :] = sj
            col_max.append(jnp.max(sj, axis=0, keepdims=True))
            yield
        if i > MOBA_TOPK:
            gate = _dot_nt(k_mean, qh)
            g = [gate[j:j + 1, :] for j in range(i)]
            bias = []
            for j in range(i):
                cnt = jnp.zeros_like(g[j])
                for jp in range(i):
                    if jp == j:
                        continue
                    ahead = (g[jp] >= g[j]) if jp < j else (g[jp] > g[j])
                    cnt = cnt + jnp.where(ahead, 1.0, 0.0)
                bias.append(jnp.where(cnt < MOBA_TOPK, 0.0, NEG_INF))
        else:
            bias = [None] * i
        m = col_max[i]
        for j in range(i):
            m = jnp.maximum(m, col_max[j] if bias[j] is None else col_max[j] + bias[j])
        shifts = [-m if bias[j] is None else bias[j] - m for j in range(i)] + [-m]
        return slot, shifts

    def weighted_values(i, hd, slot, shifts):
        ones = jnp.ones((16, blk), BF16)
        acc = jnp.zeros((HEAD_DIM + 16, blk), F32)
        for j in range(i + 1):
            sj = s_ref[slot, j * blk:(j + 1) * blk, :]
            p = jnp.exp(sj + shifts[j]).astype(BF16)
            v_aug = jnp.concatenate(
                [vt_ref[hd * HEAD_DIM:(hd + 1) * HEAD_DIM, j * blk:(j + 1) * blk], ones], axis=0)
            acc = acc + _dot(v_aug, p)
            yield
        return acc[:HEAD_DIM] / acc[HEAD_DIM:HEAD_DIM + 1]

    def interleave(*stages):
        results = [None] * len(stages)
        live = list(enumerate(stages))
        while live:
            for idx, stage in list(live):
                try:
                    next(stage)
                except StopIteration as done:
                    results[idx] = done.value
                    live.remove((idx, stage))
        return results

    groups = [(i, hd) for i in range(nb) for hd in range(LANES // HEAD_DIM)]
    n_slots = s_ref.shape[0]
    pending = interleave(*[scores(*g, n % n_slots)
                           for n, g in enumerate(groups[:MOBA_LOOKAHEAD])])
    head_out = []
    for n, (i, hd) in enumerate(groups):
        stages = [weighted_values(i, hd, *pending.pop(0))]
        if n + MOBA_LOOKAHEAD < len(groups):
            ahead = n + MOBA_LOOKAHEAD
            stages.append(scores(*groups[ahead], ahead % n_slots))
        done = interleave(*stages)
        head_out.append(done[0])
        if len(done) > 1:
            pending.append(done[1])
        if hd == LANES // HEAD_DIM - 1:
            o_t = jnp.concatenate(head_out, axis=0)
            o_ref[i * blk:(i + 1) * blk, :] = o_t.T.astype(o_ref.dtype)
            head_out = []


def _moba(q, k, v_t, batch, seq):
    n_pairs = A_WIDTH // LANES
    qk_spec = pl.BlockSpec((seq, LANES), lambda b, p: (b, p))
    vt_spec = pl.BlockSpec((LANES, seq), lambda b, p: (p, b))
    return pl.pallas_call(
        _moba_kernel,
        grid=(batch, n_pairs),
        in_specs=[qk_spec, qk_spec, vt_spec],
        out_specs=qk_spec,
        out_shape=jax.ShapeDtypeStruct(q.shape, BF16),
        scratch_shapes=[pltpu.VMEM((MOBA_LOOKAHEAD + 1, seq, MOBA_BLOCK), F32)],
        compiler_params=_params(2),
        name="moba_attn",
    )(q, k, v_t)


def _od_mixer_kernel(x_ref, gn_ref, w_ref, wg1_ref, wg2_ref, bg_ref, og_ref, wo_ref,
                     o_ref, state_ref):
    @pl.when(pl.program_id(1) == 0)
    def _():
        state_ref[...] = jnp.zeros_like(state_ref)

    tm = x_ref.shape[0]
    c = GLA_CHUNK
    nc = tm // c
    x = x_ref[...]
    h = _rms_rows(x, gn_ref[...]).astype(BF16)
    q = _dot(h, w_ref[:, :GLA_QK_W]) * (GLA_DK ** -0.5)
    k = _dot(h, w_ref[:, GLA_QK_W:2 * GLA_QK_W])
    z = _dot(_dot(h, wg1_ref[...]).astype(BF16), wg2_ref[...]) + bg_ref[...]
    log_a = (jnp.minimum(z, 0.0) - jnp.log(1.0 + jnp.exp(-jnp.abs(z)))) * (1.0 / GLA_TAU)

    in_chunk = lax.broadcasted_iota(jnp.int32, (tm, GLA_QK_W), 0) % c
    b = log_a
    step = 1
    while step < c:
        b = b + jnp.where(in_chunk >= step, pltpu.roll(b, step, 0), 0.0)
        step *= 2
    ab = GLA_ATT_BLOCK
    t_i = lax.broadcasted_iota(jnp.int32, (ab, ab), 0)
    s_i = lax.broadcasted_iota(jnp.int32, (ab, ab), 1)
    causal = jnp.logical_and(t_i // c == s_i // c, s_i <= t_i)
    b3 = b.reshape(nc, c, GLA_QK_W)
    b_last = b3[:, c - 1:c, :]
    q_in = (q * jnp.exp(b)).astype(BF16)
    k_in = (k * jnp.exp(-b)).astype(BF16)
    k_state = (k.reshape(nc, c, GLA_QK_W) * jnp.exp(b_last - b3)).reshape(tm, GLA_QK_W)
    k_state = k_state.astype(BF16)
    decay = jnp.exp(b_last)

    def intra(hd):
        ks = slice(hd * GLA_DK, (hd + 1) * GLA_DK)
        v0 = 2 * GLA_QK_W + hd * GLA_DV
        r0 = 2 * GLA_QK_W + GLA_V_W + hd * GLA_DV
        v_h = _dot(h, w_ref[:, v0:v0 + GLA_DV]).astype(BF16)
        r_h = _dot(h, w_ref[:, r0:r0 + GLA_DV])
        o_intra = []
        for a0 in range(0, tm, ab):
            rows = slice(a0, a0 + ab)
            att = jnp.where(causal, _dot_nt(q_in[rows, ks], k_in[rows, ks]), 0.0)
            o_intra.append(_dot(att.astype(BF16), v_h[rows]))
        return v_h, r_h, jnp.concatenate(o_intra, axis=0)

    def recurrent(hd, v_h, r_h, o_intra):
        ks = slice(hd * GLA_DK, (hd + 1) * GLA_DK)
        state = state_ref[hd]
        parts = []
        for n in range(nc):
            rs = slice(n * c, (n + 1) * c)
            parts.append(o_intra[rs] + _dot_nt(q_in[rs, ks], state.astype(BF16)))
            state = decay[n][:, ks] * state + _dot_tn(v_h[rs], k_state[rs, ks])
        state_ref[hd] = state
        o = jnp.concatenate(parts, axis=0)
        o = _rms_rows(o, og_ref[:, hd * GLA_DV:(hd + 1) * GLA_DV])
        return (o * _silu(r_h)).astype(BF16)

    heads = [recurrent(hd, *intra(hd)) for hd in range(GLA_HEADS)]
    o_ref[...] = x + _dot(jnp.concatenate(heads, axis=-1), wo_ref[...])


def _od_params(norm_mix, w_in, w_g1, w_g2, b_g, o_gain, w_out):
    n_od = w_in.shape[0]
    wg1 = jnp.pad(w_g1, ((0, 0), (0, 0), (0, LANES - GLA_LOWRANK))).astype(BF16)
    wg2 = jnp.pad(w_g2, ((0, 0), (0, LANES - GLA_LOWRANK), (0, 0))).astype(BF16)
    return (norm_mix[1::2].reshape(n_od, 1, D_MODEL), w_in.astype(BF16), wg1, wg2,
            b_g.reshape(n_od, 1, GLA_QK_W), o_gain.reshape(n_od, 1, GLA_V_W),
            w_out.astype(BF16))


def _od_mixer(x, params, layer, batch, seq):
    tm = ROW_TILE
    steps = seq // tm
    row_spec = pl.BlockSpec((tm, D_MODEL), lambda b, i: (b * steps + i, 0))
    return pl.pallas_call(
        _od_mixer_kernel,
        grid=(batch, steps),
        in_specs=[row_spec] + [_layer_spec(a, layer) for a in params],
        out_specs=row_spec,
        out_shape=jax.ShapeDtypeStruct(x.shape, F32),
        scratch_shapes=[pltpu.VMEM((GLA_HEADS, GLA_DV, GLA_DK), F32)],
        compiler_params=_params(2),
        name="gla_mixer",
    )(x, *params)


def _mem_kv_kernel(m_ref, gn_ref, w_ref, kg_ref, k_out, v_out):
    h = _rms_rows(m_ref[...], gn_ref[...]).astype(BF16)
    k = _dot(h, w_ref[:, :D_MODEL])
    for hh in range(X_HEADS):
        sl = slice(hh * X_HEAD_DIM, (hh + 1) * X_HEAD_DIM)
        k_out[:, sl] = _rms_rows(k[:, sl], kg_ref[...]).astype(k_out.dtype)
    v_out[...] = _dot(h, w_ref[:, D_MODEL:]).astype(v_out.dtype)


def _mem_kv(mem, gn, w_kv, k_gain):
    t = mem.shape[0]
    depth = w_kv.shape[0]
    tm = ROW_TILE
    per_layer = lambda a: pl.BlockSpec((None,) + a.shape[1:], lambda l, i: (l, 0, 0))
    out_spec = pl.BlockSpec((None, tm, D_MODEL), lambda l, i: (l, i, 0))
    return pl.pallas_call(
        _mem_kv_kernel,
        grid=(depth, t // tm),
        in_specs=[pl.BlockSpec((tm, D_MODEL), lambda l, i: (i, 0)),
                  per_layer(gn), per_layer(w_kv), per_layer(k_gain)],
        out_specs=[out_spec, out_spec],
        out_shape=[jax.ShapeDtypeStruct((depth, t, D_MODEL), BF16)] * 2,
        compiler_params=_params(2),
        name="mem_kv",
    )(mem, gn, w_kv, k_gain)


def _xattn_ffn_body(x, gq_ref, wq_ref, qg_ref, k_ref, v_ref, wo_ref,
                    gf_ref, wgu_ref, wd_ref, o_ref):
    h = _rms_rows(x, gq_ref[...]).astype(BF16)
    q = _dot(h, wq_ref[...])
    scale = X_HEAD_DIM ** -0.5
    heads = []
    for hh in range(X_HEADS):
        sl = slice(hh * X_HEAD_DIM, (hh + 1) * X_HEAD_DIM)
        qh = _rms_rows(q[:, sl], qg_ref[...]).astype(BF16)
        s = _dot_nt(qh, k_ref[:, sl]) * scale
        e = jnp.exp(s - jnp.max(s, axis=-1, keepdims=True))
        p = (e / jnp.sum(e, axis=-1, keepdims=True)).astype(BF16)
        heads.append(_dot(p, v_ref[:, sl]).astype(BF16))
    x = x + _dot(jnp.concatenate(heads, axis=-1), wo_ref[...])

    h = _rms_rows(x, gf_ref[...]).astype(BF16)
    acc = x
    for lo, hi in FFN_SPLITS:
        g = _dot(h, wgu_ref[:, lo:hi])
        u = _dot(h, wgu_ref[:, D_FF + lo:D_FF + hi])
        acc = acc + _dot((_silu(g) * u).astype(BF16), wd_ref[lo:hi, :])
    o_ref[...] = acc


def _xattn_ffn_kernel(x_ref, *rest):
    _xattn_ffn_body(x_ref[...], *rest)


def _proj_xattn_ffn_kernel(x_ref, a_ref, g_ref, wm_ref, *rest):
    n_a = a_ref.shape[1]
    x = (x_ref[...] + _dot(a_ref[...], wm_ref[:n_a, :])
         + _dot(g_ref[...], wm_ref[n_a:, :]))
    _xattn_ffn_body(x, *rest)


def _xattn_ffn(x, mixer, p, layer, k_mem, v_mem, batch, seq):
    tm = ROW_TILE
    steps = seq // tm
    row_spec = lambda n: pl.BlockSpec((tm, n), lambda b, i: (b * steps + i, 0))
    mem_spec = pl.BlockSpec((None, N_MEM, D_MODEL), lambda b, i: (layer, b, 0))
    ls = lambda name: _layer_spec(p[name], layer)
    tail_specs = [ls("gq"), ls("w_q"), ls("q_gain"), mem_spec, mem_spec, ls("w_xo"),
                  ls("gf"), ls("w_gu"), ls("w_down")]
    tail_args = (p["gq"], p["w_q"], p["q_gain"], k_mem, v_mem, p["w_xo"],
                 p["gf"], p["w_gu"], p["w_down"])
    if mixer is None:
        body, head_specs, head_args = _xattn_ffn_kernel, [row_spec(D_MODEL)], (x,)
    else:
        a, g, w_m, m_layer = mixer
        body = _proj_xattn_ffn_kernel
        head_specs = [row_spec(D_MODEL), row_spec(a.shape[1]), row_spec(g.shape[1]),
                      _layer_spec(w_m, m_layer)]
        head_args = (x, a, g, w_m)
    return pl.pallas_call(
        body,
        grid=(batch, steps),
        in_specs=head_specs + tail_specs,
        out_specs=row_spec(D_MODEL),
        out_shape=jax.ShapeDtypeStruct(x.shape, F32),
        compiler_params=_params(2),
        name="xattn_ffn" if mixer is None else "proj_xattn_ffn",
    )(*head_args, *tail_args)


def kernel(x, mem, positions, norm_mix, norm_mem_q, norm_mem_kv, norm_ffn, ev_w_in, ev_q_gain, ev_k_gain, ev_w_s, ev_b_s, ev_ln_g, ev_ln_b, ev_w_out, od_w_in, od_w_g1, od_w_g2, od_b_g, od_o_gain, od_w_out, xa_w_q, xa_w_kv, xa_q_gain, xa_k_gain, xa_w_out, ffn_w_gu, ffn_w_down):
    batch, seq, d_model = x.shape
    assert d_model == D_MODEL and seq % ROW_TILE == 0 and seq % MOBA_BLOCK == 0
    assert mem.shape == (batch, N_MEM, D_MODEL)
    xf = x.reshape(batch * seq, D_MODEL)
    memf = mem.reshape(batch * N_MEM, D_MODEL)
    cos_t, sin_t = _rope_tables(positions)
    vec = lambda a: a.reshape(a.shape[0], 1, a.shape[1])
    ev = _ev_params(norm_mix, ev_w_in, ev_q_gain, ev_k_gain, ev_ln_g, ev_ln_b, ev_w_s, ev_b_s)
    ev_w_out_bf = ev_w_out.astype(BF16)
    od = _od_params(norm_mix, od_w_in, od_w_g1, od_w_g2, od_b_g, od_o_gain, od_w_out)
    post = dict(gq=vec(norm_mem_q), w_q=xa_w_q.astype(BF16), q_gain=vec(xa_q_gain),
                w_xo=xa_w_out.astype(BF16), gf=vec(norm_ffn),
                w_gu=ffn_w_gu.astype(BF16), w_down=ffn_w_down.astype(BF16))
    k_mem, v_mem = _mem_kv(memf, vec(norm_mem_kv), xa_w_kv.astype(BF16), vec(xa_k_gain))
    for l in range(DEPTH):
        i = l // 2
        if l % 2 == 0:
            q, k, v_t, g = _ev_proj(xf, ev, i, cos_t, sin_t)
            mixer = (_moba(q, k, v_t, batch, seq), g, ev_w_out_bf, i)
        else:
            xf = _od_mixer(xf, od, i, batch, seq)
            mixer = None
        xf = _xattn_ffn(xf, mixer, post, l, k_mem, v_mem, batch, seq)
    return xf.reshape(batch, seq, D_MODEL)
```

```python
import math

import jax
import jax.numpy as jnp
from jax import lax
from jax.experimental import pallas as pl
from jax.experimental.pallas import tpu as pltpu

F32 = jnp.float32
BF16 = jnp.bfloat16

D_MODEL = 1024
DEPTH = 4
N_MEM = 256
EPS = 1e-6
NEG_INF = -1e30

HEAD_DIM = 64
ROPE_DIM = HEAD_DIM // 4
ROPE_THETA = 500000.0

A_HEADS = 8
A_WIDTH = A_HEADS * HEAD_DIM
MOBA_BLOCK = 256
MOBA_TOPK = 3
MOBA_LOOKAHEAD = 3

B_GROUPS = 8
B_WIDTH = B_GROUPS * HEAD_DIM
GMLP_CHUNK = 128

GLA_HEADS = 4
GLA_DK = (D_MODEL // 2) // GLA_HEADS
GLA_DV = D_MODEL // GLA_HEADS
GLA_QK_W = GLA_HEADS * GLA_DK
GLA_V_W = GLA_HEADS * GLA_DV
GLA_LOWRANK = 16
GLA_TAU = 16.0
GLA_CHUNK = 64
GLA_ATT_BLOCK = 256

X_HEADS = 4
X_HEAD_DIM = D_MODEL // X_HEADS

D_FF = -(-8 * D_MODEL // (3 * 256)) * 256

LANES = 128
VMEM_LIMIT_BYTES = 56 * 1024 * 1024

ROW_TILE = 512
FFN_SPLITS = ((0, 1536), (1536, D_FF))


def _params(n_axes):
    return pltpu.CompilerParams(
        dimension_semantics=("arbitrary",) * n_axes,
        vmem_limit_bytes=VMEM_LIMIT_BYTES)


def _layer_spec(stacked, layer):
    nd = stacked.ndim - 1
    return pl.BlockSpec((None,) + stacked.shape[1:], lambda *_: (layer,) + (0,) * nd,
                        pipeline_mode=pl.Buffered(1))


def _dot(a, b):
    return jnp.dot(a, b, preferred_element_type=F32)


def _dot_nt(a, b):
    return lax.dot_general(a, b, (((1,), (1,)), ((), ())),
                           preferred_element_type=F32)


def _dot_tn(a, b):
    return lax.dot_general(a, b, (((0,), (0,)), ((), ())),
                           preferred_element_type=F32)


def _rms_rows(x, gain):
    ms = jnp.mean(x * x, axis=-1, keepdims=True)
    return x * lax.rsqrt(ms + EPS) * gain


def _half_lane_mean(x, lo):
    s_lo = jnp.sum(jnp.where(lo, x, 0.0), axis=-1, keepdims=True)
    s_hi = jnp.sum(jnp.where(lo, 0.0, x), axis=-1, keepdims=True)
    return jnp.where(lo, s_lo, s_hi) * (1.0 / HEAD_DIM)


def _gelu(x):
    return 0.5 * x * (1.0 + lax.erf(x * math.sqrt(0.5)))


def _silu(x):
    return x / (1.0 + jnp.exp(-x))


def _rope_table_kernel(pos_ref, invf_ref, sign_ref, cos_ref, sin_ref):
    ang = pos_ref[...] * invf_ref[...]
    cos_ref[...] = jnp.cos(ang)
    sin_ref[...] = jnp.sin(ang) * sign_ref[...]


def _rope_tables(positions):
    t = positions.size
    half = ROPE_DIM // 2
    per_row = LANES // ROPE_DIM
    inv_freq = ROPE_THETA ** (-jnp.arange(half, dtype=F32) * 2.0 / ROPE_DIM)
    d = jnp.arange(LANES) % ROPE_DIM
    invf = inv_freq[d % half][None, :]
    sign = jnp.where(d < half, -1.0, 1.0).astype(F32)[None, :]
    pos = jnp.repeat(positions.astype(F32).reshape(t // per_row, per_row), ROPE_DIM, axis=1)
    full = pl.BlockSpec((t // per_row, LANES), lambda: (0, 0))
    row = pl.BlockSpec((1, LANES), lambda: (0, 0))
    cos_p, sin_p = pl.pallas_call(
        _rope_table_kernel,
        in_specs=[full, row, row],
        out_specs=[full, full],
        out_shape=[jax.ShapeDtypeStruct((t // per_row, LANES), F32)] * 2,
        name="rope_tables",
    )(pos, invf, sign)

    def per_token(packed, fill):
        rot = packed.reshape(t, ROPE_DIM)
        rest = jnp.full((t, HEAD_DIM - ROPE_DIM), fill, F32)
        return jnp.tile(jnp.concatenate([rot, rest], axis=1), (1, LANES // HEAD_DIM))

    return per_token(cos_p, 1.0), per_token(sin_p, 0.0)


def _ev_proj_kernel(x_ref, gn_ref, wt_ref, wk_ref, qg_ref, kg_ref, cos_ref, sin_ref,
                    lng_ref, lnb_ref, wst_ref, bs_ref,
                    qt_out, k_out, vt_out, g_out):
    tm = x_ref.shape[0]
    h = _rms_rows(x_ref[...], gn_ref[...]).astype(BF16)
    cos = cos_ref[...]
    sin = sin_ref[...]
    half = ROPE_DIM // 2

    def section_t(row0, width):
        return _dot_nt(wt_ref[row0:row0 + width, :], h)

    cos_t = cos.T[:half]
    sin_t = sin.T[half:ROPE_DIM]
    q_t = section_t(0, A_WIDTH)
    for hd in range(A_HEADS):
        xq = q_t[hd * HEAD_DIM:(hd + 1) * HEAD_DIM]
        ms = jnp.mean(xq * xq, axis=0, keepdims=True)
        y = xq * lax.rsqrt(ms + EPS) * qg_ref[...]
        x1, x2 = y[:half], y[half:ROPE_DIM]
        y = jnp.concatenate([x1 * cos_t - x2 * sin_t, x2 * cos_t + x1 * sin_t, y[ROPE_DIM:]],
                            axis=0)
        qt_out[hd * HEAD_DIM:(hd + 1) * HEAD_DIM, :] = y.astype(qt_out.dtype)

    lane = lax.broadcasted_iota(jnp.int32, (1, LANES), 1)
    lo = lane < HEAD_DIM
    first = (lane % HEAD_DIM) < half
    p = _dot(h, wk_ref[...])
    for c in range(A_WIDTH // LANES):
        xc = p[:, c * LANES:(c + 1) * LANES]
        y = xc * lax.rsqrt(_half_lane_mean(xc * xc, lo) + EPS) * kg_ref[...]
        partner = jnp.where(first, pltpu.roll(y, LANES - half, 1), pltpu.roll(y, half, 1))
        k_out[:, c * LANES:(c + 1) * LANES] = (y * cos + partner * sin).astype(k_out.dtype)

    vt_out[...] = section_t(2 * A_WIDTH, A_WIDTH).astype(vt_out.dtype)

    u_t = _gelu(section_t(3 * A_WIDTH, B_WIDTH))
    vg_t = _gelu(section_t(3 * A_WIDTH + B_WIDTH, B_WIDTH))
    n_chunks = tm // GMLP_CHUNK
    j_i = lax.broadcasted_iota(jnp.int32, (GMLP_CHUNK, GMLP_CHUNK), 0)
    i_i = lax.broadcasted_iota(jnp.int32, (GMLP_CHUNK, GMLP_CHUNK), 1)
    keep = j_i <= i_i
    g_rows = []
    for g in range(B_GROUPS):
        rows = slice(g * HEAD_DIM, (g + 1) * HEAD_DIM)
        xg = vg_t[rows]
        xg = xg - jnp.mean(xg, axis=0, keepdims=True)
        var = jnp.mean(xg * xg, axis=0, keepdims=True)
        vn = (xg * lax.rsqrt(var + EPS) * lng_ref[rows] + lnb_ref[rows]).astype(BF16)
        stacked = jnp.concatenate(
            [vn[:, r * GMLP_CHUNK:(r + 1) * GMLP_CHUNK] for r in range(n_chunks)], axis=0)
        w_t = jnp.where(keep, wst_ref[g], 0.0).astype(BF16)
        mixed = _dot(stacked, w_t) + bs_ref[g:g + 1, :]
        mixed = jnp.concatenate(
            [mixed[r * HEAD_DIM:(r + 1) * HEAD_DIM] for r in range(n_chunks)], axis=1)
        g_rows.append(u_t[rows] * mixed)
    g_out[...] = jnp.concatenate(g_rows, axis=0).T.astype(g_out.dtype)


def _ev_params(norm_mix, w_in, q_gain, k_gain, ln_g, ln_b, w_s, b_s):
    n_ev = w_in.shape[0]
    tm = ROW_TILE
    along_lanes = lambda a: jnp.broadcast_to(a[:, :, None], a.shape + (tm,))
    gn = norm_mix[0::2].reshape(n_ev, 1, D_MODEL)
    w_t = jnp.swapaxes(w_in, 1, 2).astype(BF16)
    w_k = w_in[:, :, A_WIDTH:2 * A_WIDTH].astype(BF16)
    qg = along_lanes(q_gain)
    kg = jnp.tile(k_gain, (1, LANES // HEAD_DIM)).reshape(n_ev, 1, LANES)
    lng = along_lanes(ln_g.reshape(n_ev, B_WIDTH))
    lnb = along_lanes(ln_b.reshape(n_ev, B_WIDTH))
    w_s_t = jnp.swapaxes(w_s, 2, 3)
    return (gn, w_t, w_k, qg, kg), (lng, lnb, w_s_t, b_s)


def _ev_proj(x, params, layer, cos_t, sin_t):
    t = x.shape[0]
    tm = ROW_TILE
    row_spec = lambda n: pl.BlockSpec((tm, n), lambda i: (i, 0))
    col_spec = pl.BlockSpec((A_WIDTH, tm), lambda i: (0, i))
    head, tail = params
    return pl.pallas_call(
        _ev_proj_kernel,
        grid=(t // tm,),
        in_specs=([row_spec(D_MODEL)] + [_layer_spec(a, layer) for a in head]
                  + [row_spec(LANES), row_spec(LANES)] + [_layer_spec(a, layer) for a in tail]),
        out_specs=[col_spec, row_spec(A_WIDTH), col_spec, row_spec(B_WIDTH)],
        out_shape=[jax.ShapeDtypeStruct((A_WIDTH, t), BF16),
                   jax.ShapeDtypeStruct((t, A_WIDTH), BF16),
                   jax.ShapeDtypeStruct((A_WIDTH, t), BF16),
                   jax.ShapeDtypeStruct((t, B_WIDTH), BF16)],
        compiler_params=_params(1),
        name="ev_proj",
    )(x, *head, cos_t, sin_t, *tail)


def _moba_kernel(qt_ref, k_ref, vt_ref, o_ref, s_ref):
    seq = k_ref.shape[0]
    nb = seq // MOBA_BLOCK
    blk = MOBA_BLOCK
    scale = HEAD_DIM ** -0.5

    k_mean = jnp.mean(k_ref[...].astype(F32).reshape(nb, blk, LANES), axis=1)
    k_mean = jnp.concatenate([k_mean, jnp.zeros((16 - nb, LANES), F32)], axis=0).astype(BF16)

    lo = lax.broadcasted_iota(jnp.int32, (LANES, blk), 0) < HEAD_DIM
    key_i = lax.broadcasted_iota(jnp.int32, (blk, blk), 0)
    qry_i = lax.broadcasted_iota(jnp.int32, (blk, blk), 1)
    causal = key_i <= qry_i

    def scores(i, hd, slot):
        q_i = qt_ref[:, i * blk:(i + 1) * blk]
        head_mask = lo if hd == 0 else jnp.logical_not(lo)
        qh = jnp.where(head_mask, q_i, jnp.zeros_like(q_i)) * scale
        col_max = []
        for j in range(i + 1):
            sj = _dot(k_ref[j * blk:(j + 1) * blk, :], qh)
            if j == i:
                sj = jnp.where(causal, sj, NEG_INF)
            s_ref[slot, j * blk:(j + 1) * blk, :] = sj
            col_max.append(jnp.max(sj, axis=0, keepdims=True))
            yield
        if i > MOBA_TOPK:
            gate = _dot(k_mean, qh)
            g = [gate[j:j + 1, :] for j in range(i)]
            bias = []
            for j in range(i):
                cnt = jnp.zeros_like(g[j])
                for jp in range(i):
                    if jp == j:
                        continue
                    ahead = (g[jp] >= g[j]) if jp < j else (g[jp] > g[j])
                    cnt = cnt + jnp.where(ahead, 1.0, 0.0)
                bias.append(jnp.where(cnt < MOBA_TOPK, 0.0, NEG_INF))
        else:
            bias = [None] * i
        m = col_max[i]
        for j in range(i):
            m = jnp.maximum(m, col_max[j] if bias[j] is None else col_max[j] + bias[j])
        shifts = [-m if bias[j] is None else bias[j] - m for j in range(i)] + [-m]
        return slot, shifts

    def weighted_values(i, hd, slot, shifts):
        ones = jnp.ones((16, blk), BF16)
        acc = jnp.zeros((HEAD_DIM + 16, blk), F32)
        for j in range(i + 1):
            sj = s_ref[slot, j * blk:(j + 1) * blk, :]
            p = jnp.exp(sj + shifts[j]).astype(BF16)
            v_aug = jnp.concatenate(
                [vt_ref[hd * HEAD_DIM:(hd + 1) * HEAD_DIM, j * blk:(j + 1) * blk], ones], axis=0)
            acc = acc + _dot(v_aug, p)
            yield
        return acc[:HEAD_DIM] / acc[HEAD_DIM:HEAD_DIM + 1]

    def interleave(*stages):
        results = [None] * len(stages)
        live = list(enumerate(stages))
        while live:
            for idx, stage in list(live):
                try:
                    next(stage)
                except StopIteration as done:
                    results[idx] = done.value
                    live.remove((idx, stage))
        return results

    groups = [(i, hd) for i in range(nb) for hd in range(LANES // HEAD_DIM)]
    n_slots = s_ref.shape[0]
    pending = interleave(*[scores(*g, n % n_slots)
                           for n, g in enumerate(groups[:MOBA_LOOKAHEAD])])
    head_out = []
    for n, (i, hd) in enumerate(groups):
        stages = [weighted_values(i, hd, *pending.pop(0))]
        if n + MOBA_LOOKAHEAD < len(groups):
            ahead = n + MOBA_LOOKAHEAD
            stages.append(scores(*groups[ahead], ahead % n_slots))
        done = interleave(*stages)
        head_out.append(done[0])
        if len(done) > 1:
            pending.append(done[1])
        if hd == LANES // HEAD_DIM - 1:
            o_t = jnp.concatenate(head_out, axis=0)
            o_ref[i * blk:(i + 1) * blk, :] = o_t.T.astype(o_ref.dtype)
            head_out = []


def _moba(q_t, k, v_t, batch, seq):
    n_pairs = A_WIDTH // LANES
    row_major = pl.BlockSpec((seq, LANES), lambda b, p: (b, p))
    feat_major = pl.BlockSpec((LANES, seq), lambda b, p: (p, b))
    return pl.pallas_call(
        _moba_kernel,
        grid=(batch, n_pairs),
        in_specs=[feat_major, row_major, feat_major],
        out_specs=row_major,
        out_shape=jax.ShapeDtypeStruct(k.shape, BF16),
        scratch_shapes=[pltpu.VMEM((MOBA_LOOKAHEAD + 1, seq, MOBA_BLOCK), F32)],
        compiler_params=_params(2),
        name="moba_attn",
    )(q_t, k, v_t)


def _od_mixer_kernel(x_ref, gn_ref, w_ref, wg1_ref, wg2_ref, bg_ref, og_ref, wo_ref,
                     o_ref, state_ref):
    @pl.when(pl.program_id(1) == 0)
    def _():
        state_ref[...] = jnp.zeros_like(state_ref)

    tm = x_ref.shape[0]
    c = GLA_CHUNK
    nc = tm // c
    x = x_ref[...]
    h = _rms_rows(x, gn_ref[...]).astype(BF16)
    q = _dot(h, w_ref[:, :GLA_QK_W]) * (GLA_DK ** -0.5)
    k = _dot(h, w_ref[:, GLA_QK_W:2 * GLA_QK_W])
    z = _dot(_dot(h, wg1_ref[...]).astype(BF16), wg2_ref[...]) + bg_ref[...]
    log_a = (jnp.minimum(z, 0.0) - jnp.log(1.0 + jnp.exp(-jnp.abs(z)))) * (1.0 / GLA_TAU)

    in_chunk = lax.broadcasted_iota(jnp.int32, (tm, GLA_QK_W), 0) % c
    b = log_a
    step = 1
    while step < c:
        b = b + jnp.where(in_chunk >= step, pltpu.roll(b, step, 0), 0.0)
        step *= 2
    ab = GLA_ATT_BLOCK
    t_i = lax.broadcasted_iota(jnp.int32, (ab, ab), 0)
    s_i = lax.broadcasted_iota(jnp.int32, (ab, ab), 1)
    causal = jnp.logical_and(t_i // c == s_i // c, s_i <= t_i)
    b3 = b.reshape(nc, c, GLA_QK_W)
    b_last = b3[:, c - 1:c, :]
    q_in = (q * jnp.exp(b)).astype(BF16)
    k_in = (k * jnp.exp(-b)).astype(BF16)
    k_state = (k.reshape(nc, c, GLA_QK_W) * jnp.exp(b_last - b3)).reshape(tm, GLA_QK_W)
    k_state = k_state.astype(BF16)
    decay = jnp.exp(b_last)

    def intra(hd):
        ks = slice(hd * GLA_DK, (hd + 1) * GLA_DK)
        v0 = 2 * GLA_QK_W + hd * GLA_DV
        r0 = 2 * GLA_QK_W + GLA_V_W + hd * GLA_DV
        v_h = _dot(h, w_ref[:, v0:v0 + GLA_DV]).astype(BF16)
        r_h = _dot(h, w_ref[:, r0:r0 + GLA_DV])
        o_intra = []
        for a0 in range(0, tm, ab):
            rows = slice(a0, a0 + ab)
            att = jnp.where(causal, _dot_nt(q_in[rows, ks], k_in[rows, ks]), 0.0)
            o_intra.append(_dot(att.astype(BF16), v_h[rows]))
        return v_h, r_h, jnp.concatenate(o_intra, axis=0)

    def recurrent(hd, v_h, r_h, o_intra):
        ks = slice(hd * GLA_DK, (hd + 1) * GLA_DK)
        state = state_ref[hd]
        parts = []
        for n in range(nc):
            rs = slice(n * c, (n + 1) * c)
            parts.append(o_intra[rs] + _dot_nt(q_in[rs, ks], state.astype(BF16)))
            state = decay[n][:, ks] * state + _dot_tn(v_h[rs], k_state[rs, ks])
        state_ref[hd] = state
        o = jnp.concatenate(parts, axis=0)
        o = _rms_rows(o, og_ref[:, hd * GLA_DV:(hd + 1) * GLA_DV])
        return (o * _silu(r_h)).astype(BF16)

    heads = [recurrent(hd, *intra(hd)) for hd in range(GLA_HEADS)]
    o_ref[...] = x + _dot(jnp.concatenate(heads, axis=-1), wo_ref[...])


def _od_params(norm_mix, w_in, w_g1, w_g2, b_g, o_gain, w_out):
    n_od = w_in.shape[0]
    wg1 = jnp.pad(w_g1, ((0, 0), (0, 0), (0, LANES - GLA_LOWRANK))).astype(BF16)
    wg2 = jnp.pad(w_g2, ((0, 0), (0, LANES - GLA_LOWRANK), (0, 0))).astype(BF16)
    return (norm_mix[1::2].reshape(n_od, 1, D_MODEL), w_in.astype(BF16), wg1, wg2,
            b_g.reshape(n_od, 1, GLA_QK_W), o_gain.reshape(n_od, 1, GLA_V_W),
            w_out.astype(BF16))


def _od_mixer(x, params, layer, batch, seq):
    tm = ROW_TILE
    steps = seq // tm
    row_spec = pl.BlockSpec((tm, D_MODEL), lambda b, i: (b * steps + i, 0))
    return pl.pallas_call(
        _od_mixer_kernel,
        grid=(batch, steps),
        in_specs=[row_spec] + [_layer_spec(a, layer) for a in params],
        out_specs=row_spec,
        out_shape=jax.ShapeDtypeStruct(x.shape, F32),
        scratch_shapes=[pltpu.VMEM((GLA_HEADS, GLA_DV, GLA_DK), F32)],
        compiler_params=_params(2),
        name="gla_mixer",
    )(x, *params)


def _mem_kv_kernel(m_ref, gn_ref, w_ref, kg_ref, k_out, v_out):
    h = _rms_rows(m_ref[...], gn_ref[...]).astype(BF16)
    k = _dot(h, w_ref[:, :D_MODEL])
    for hh in range(X_HEADS):
        sl = slice(hh * X_HEAD_DIM, (hh + 1) * X_HEAD_DIM)
        k_out[:, sl] = _rms_rows(k[:, sl], kg_ref[...]).astype(k_out.dtype)
    v_out[...] = _dot(h, w_ref[:, D_MODEL:]).astype(v_out.dtype)


def _mem_kv(mem, gn, w_kv, k_gain):
    t = mem.shape[0]
    depth = w_kv.shape[0]
    tm = ROW_TILE
    per_layer = lambda a: pl.BlockSpec((None,) + a.shape[1:], lambda l, i: (l, 0, 0))
    out_spec = pl.BlockSpec((None, tm, D_MODEL), lambda l, i: (l, i, 0))
    return pl.pallas_call(
        _mem_kv_kernel,
        grid=(depth, t // tm),
        in_specs=[pl.BlockSpec((tm, D_MODEL), lambda l, i: (i, 0)),
                  per_layer(gn), per_layer(w_kv), per_layer(k_gain)],
        out_specs=[out_spec, out_spec],
        out_shape=[jax.ShapeDtypeStruct((depth, t, D_MODEL), BF16)] * 2,
        compiler_params=_params(2),
        name="mem_kv",
    )(mem, gn, w_kv, k_gain)


def _xattn_ffn_body(x, gq_ref, wq_ref, qg_ref, k_ref, v_ref, wo_ref,
                    gf_ref, wgu_ref, wd_ref, o_ref):
    h = _rms_rows(x, gq_ref[...]).astype(BF16)
    q = _dot(h, wq_ref[...])
    scale = X_HEAD_DIM ** -0.5
    heads = []
    for hh in range(X_HEADS):
        sl = slice(hh * X_HEAD_DIM, (hh + 1) * X_HEAD_DIM)
        qh = _rms_rows(q[:, sl], qg_ref[...]).astype(BF16)
        s = _dot_nt(qh, k_ref[:, sl]) * scale
        e = jnp.exp(s - jnp.max(s, axis=-1, keepdims=True))
        p = (e / jnp.sum(e, axis=-1, keepdims=True)).astype(BF16)
        heads.append(_dot(p, v_ref[:, sl]).astype(BF16))
    x = x + _dot(jnp.concatenate(heads, axis=-1), wo_ref[...])

    h = _rms_rows(x, gf_ref[...]).astype(BF16)
    acc = x
    for lo, hi in FFN_SPLITS:
        g = _dot(h, wgu_ref[:, lo:hi])
        u = _dot(h, wgu_ref[:, D_FF + lo:D_FF + hi])
        acc = acc + _dot((_silu(g) * u).astype(BF16), wd_ref[lo:hi, :])
    o_ref[...] = acc


def _xattn_ffn_kernel(x_ref, *rest):
    _xattn_ffn_body(x_ref[...], *rest)


def _proj_xattn_ffn_kernel(x_ref, a_ref, g_ref, wm_ref, *rest):
    n_a = a_ref.shape[1]
    x = (x_ref[...] + _dot(a_ref[...], wm_ref[:n_a, :])
         + _dot(g_ref[...], wm_ref[n_a:, :]))
    _xattn_ffn_body(x, *rest)


def _xattn_ffn(x, mixer, p, layer, k_mem, v_mem, batch, seq):
    tm = ROW_TILE
    steps = seq // tm
    row_spec = lambda n: pl.BlockSpec((tm, n), lambda b, i: (b * steps + i, 0))
    mem_spec = pl.BlockSpec((None, N_MEM, D_MODEL), lambda b, i: (layer, b, 0))
    ls = lambda name: _layer_spec(p[name], layer)
    tail_specs = [ls("gq"), ls("w_q"), ls("q_gain"), mem_spec, mem_spec, ls("w_xo"),
                  ls("gf"), ls("w_gu"), ls("w_down")]
    tail_args = (p["gq"], p["w_q"], p["q_gain"], k_mem, v_mem, p["w_xo"],
                 p["gf"], p["w_gu"], p["w_down"])
    if mixer is None:
        body, head_specs, head_args = _xattn_ffn_kernel, [row_spec(D_MODEL)], (x,)
    else:
        a, g, w_m, m_layer = mixer
        body = _proj_xattn_ffn_kernel
        head_specs = [row_spec(D_MODEL), row_spec(a.shape[1]), row_spec(g.shape[1]),
                      _layer_spec(w_m, m_layer)]
        head_args = (x, a, g, w_m)
    return pl.pallas_call(
        body,
        grid=(batch, steps),
        in_specs=head_specs + tail_specs,
        out_specs=row_spec(D_MODEL),
        out_shape=jax.ShapeDtypeStruct(x.shape, F32),
        compiler_params=_params(2),
        name="xattn_ffn" if mixer is None else "proj_xattn_ffn",
    )(*head_args, *tail_args)


def kernel(x, mem, positions, norm_mix, norm_mem_q, norm_mem_kv, norm_ffn, ev_w_in, ev_q_gain, ev_k_gain, ev_w_s, ev_b_s, ev_ln_g, ev_ln_b, ev_w_out, od_w_in, od_w_g1, od_w_g2, od_b_g, od_o_gain, od_w_out, xa_w_q, xa_w_kv, xa_q_gain, xa_k_gain, xa_w_out, ffn_w_gu, ffn_w_down):
    batch, seq, d_model = x.shape
    assert d_model == D_MODEL and seq % ROW_TILE == 0 and seq % MOBA_BLOCK == 0
    assert mem.shape == (batch, N_MEM, D_MODEL)
    xf = x.reshape(batch * seq, D_MODEL)
    memf = mem.reshape(batch * N_MEM, D_MODEL)
    cos_t, sin_t = _rope_tables(positions)
    vec = lambda a: a.reshape(a.shape[0], 1, a.shape[1])
    ev = _ev_params(norm_mix, ev_w_in, ev_q_gain, ev_k_gain, ev_ln_g, ev_ln_b, ev_w_s, ev_b_s)
    ev_w_out_bf = ev_w_out.astype(BF16)
    od = _od_params(norm_mix, od_w_in, od_w_g1, od_w_g2, od_b_g, od_o_gain, od_w_out)
    post = dict(gq=vec(norm_mem_q), w_q=xa_w_q.astype(BF16), q_gain=vec(xa_q_gain),
                w_xo=xa_w_out.astype(BF16), gf=vec(norm_ffn),
                w_gu=ffn_w_gu.astype(BF16), w_down=ffn_w_down.astype(BF16))
    k_mem, v_mem = _mem_kv(memf, vec(norm_mem_kv), xa_w_kv.astype(BF16), vec(xa_k_gain))
    for l in range(DEPTH):
        i = l // 2
        if l % 2 == 0:
            q_t, k, v_t, g = _ev_proj(xf, ev, i, cos_t, sin_t)
            mixer = (_moba(q_t, k, v_t, batch, seq), g, ev_w_out_bf, i)
        else:
            xf = _od_mixer(xf, od, i, batch, seq)
            mixer = None
        xf = _xattn_ffn(xf, mixer, post, l, k_mem, v_mem, batch, seq)
    return xf.reshape(batch, seq, D_MODEL)
```

```python
import math
from functools import partial

import jax
import jax.numpy as jnp
from jax import lax
from jax.experimental import pallas as pl
from jax.experimental.pallas import tpu as pltpu

F32 = jnp.float32
BF16 = jnp.bfloat16

D_MODEL = 1024
DEPTH = 4
N_MEM = 256
EPS = 1e-6
NEG_INF = -1e30

HEAD_DIM = 64
ROPE_DIM = HEAD_DIM // 4
ROPE_THETA = 500000.0

A_HEADS = 8
A_WIDTH = A_HEADS * HEAD_DIM
MOBA_BLOCK = 256
MOBA_TOPK = 3
MOBA_LOOKAHEAD = 3

B_GROUPS = 8
B_WIDTH = B_GROUPS * HEAD_DIM
GMLP_CHUNK = 128

GLA_HEADS = 4
GLA_DK = (D_MODEL // 2) // GLA_HEADS
GLA_DV = D_MODEL // GLA_HEADS
GLA_QK_W = GLA_HEADS * GLA_DK
GLA_V_W = GLA_HEADS * GLA_DV
GLA_LOWRANK = 16
GLA_TAU = 16.0
GLA_CHUNK = 64
GLA_ATT_BLOCK = 256

X_HEADS = 4
X_HEAD_DIM = D_MODEL // X_HEADS

D_FF = -(-8 * D_MODEL // (3 * 256)) * 256

LANES = 128
VMEM_LIMIT_BYTES = 56 * 1024 * 1024

ROW_TILE = 512
FFN_SPLITS = tuple((lo, lo + 256) for lo in range(0, D_FF, 256))


def _params(n_axes):
    return pltpu.CompilerParams(
        dimension_semantics=("arbitrary",) * n_axes,
        vmem_limit_bytes=VMEM_LIMIT_BYTES)


def _layer_spec(stacked, layer):
    nd = stacked.ndim - 1
    return pl.BlockSpec((None,) + stacked.shape[1:], lambda *_: (layer,) + (0,) * nd,
                        pipeline_mode=pl.Buffered(1))


def _dot(a, b):
    return jnp.dot(a, b, preferred_element_type=F32)


def _dot_nt(a, b):
    return lax.dot_general(a, b, (((1,), (1,)), ((), ())),
                           preferred_element_type=F32)


def _dot_tn(a, b):
    return lax.dot_general(a, b, (((0,), (0,)), ((), ())),
                           preferred_element_type=F32)


def _rms_rows(x, gain):
    ms = jnp.mean(x * x, axis=-1, keepdims=True)
    return x * lax.rsqrt(ms + EPS) * gain


def _half_lane_mean(x, lo):
    s_lo = jnp.sum(jnp.where(lo, x, 0.0), axis=-1, keepdims=True)
    s_hi = jnp.sum(jnp.where(lo, 0.0, x), axis=-1, keepdims=True)
    return jnp.where(lo, s_lo, s_hi) * (1.0 / HEAD_DIM)


def _gelu(x):
    return 0.5 * x * (1.0 + lax.erf(x * math.sqrt(0.5)))


def _silu(x):
    return x / (1.0 + jnp.exp(-x))


def _rope_table_kernel(pos_ref, invf_ref, sign_ref, cos_ref, sin_ref):
    ang = pos_ref[...] * invf_ref[...]
    cos_ref[...] = jnp.cos(ang)
    sin_ref[...] = jnp.sin(ang) * sign_ref[...]


def _rope_tables(positions):
    t = positions.size
    half = ROPE_DIM // 2
    per_row = LANES // ROPE_DIM
    inv_freq = ROPE_THETA ** (-jnp.arange(half, dtype=F32) * 2.0 / ROPE_DIM)
    d = jnp.arange(LANES) % ROPE_DIM
    invf = inv_freq[d % half][None, :]
    sign = jnp.where(d < half, -1.0, 1.0).astype(F32)[None, :]
    pos = jnp.repeat(positions.astype(F32).reshape(t // per_row, per_row), ROPE_DIM, axis=1)
    full = pl.BlockSpec((t // per_row, LANES), lambda: (0, 0))
    row = pl.BlockSpec((1, LANES), lambda: (0, 0))
    cos_p, sin_p = pl.pallas_call(
        _rope_table_kernel,
        in_specs=[full, row, row],
        out_specs=[full, full],
        out_shape=[jax.ShapeDtypeStruct((t // per_row, LANES), F32)] * 2,
        name="rope_tables",
    )(pos, invf, sign)

    def per_token(packed, fill):
        rot = packed.reshape(t, ROPE_DIM)
        rest = jnp.full((t, HEAD_DIM - ROPE_DIM), fill, F32)
        return jnp.tile(jnp.concatenate([rot, rest], axis=1), (1, LANES // HEAD_DIM))

    return per_token(cos_p, 1.0), per_token(sin_p, 0.0)


def _ev_proj_kernel(x_ref, gn_ref, wt_ref, wk_ref, qg_ref, kg_ref, cos_ref, sin_ref,
                    lng_ref, lnb_ref, wst_ref, bs_ref,
                    qt_out, k_out, vt_out, g_out):
    tm = x_ref.shape[0]
    h = _rms_rows(x_ref[...], gn_ref[...]).astype(BF16)
    cos = cos_ref[...]
    sin = sin_ref[...]
    half = ROPE_DIM // 2

    def section_t(row0, width):
        return _dot_nt(wt_ref[row0:row0 + width, :], h)

    cos_t = cos.T[:half]
    sin_t = sin.T[half:ROPE_DIM]
    q_t = section_t(0, A_WIDTH)
    for hd in range(A_HEADS):
        xq = q_t[hd * HEAD_DIM:(hd + 1) * HEAD_DIM]
        ms = jnp.mean(xq * xq, axis=0, keepdims=True)
        y = xq * lax.rsqrt(ms + EPS) * qg_ref[...]
        x1, x2 = y[:half], y[half:ROPE_DIM]
        y = jnp.concatenate([x1 * cos_t - x2 * sin_t, x2 * cos_t + x1 * sin_t, y[ROPE_DIM:]],
                            axis=0)
        qt_out[hd * HEAD_DIM:(hd + 1) * HEAD_DIM, :] = y.astype(qt_out.dtype)

    lane = lax.broadcasted_iota(jnp.int32, (1, LANES), 1)
    lo = lane < HEAD_DIM
    first = (lane % HEAD_DIM) < half
    p = _dot(h, wk_ref[...])
    for c in range(A_WIDTH // LANES):
        xc = p[:, c * LANES:(c + 1) * LANES]
        y = xc * lax.rsqrt(_half_lane_mean(xc * xc, lo) + EPS) * kg_ref[...]
        partner = jnp.where(first, pltpu.roll(y, LANES - half, 1), pltpu.roll(y, half, 1))
        k_out[:, c * LANES:(c + 1) * LANES] = (y * cos + partner * sin).astype(k_out.dtype)

    vt_out[...] = section_t(2 * A_WIDTH, A_WIDTH).astype(vt_out.dtype)

    u_t = _gelu(section_t(3 * A_WIDTH, B_WIDTH))
    vg_t = _gelu(section_t(3 * A_WIDTH + B_WIDTH, B_WIDTH))
    n_chunks = tm // GMLP_CHUNK
    j_i = lax.broadcasted_iota(jnp.int32, (GMLP_CHUNK, GMLP_CHUNK), 0)
    i_i = lax.broadcasted_iota(jnp.int32, (GMLP_CHUNK, GMLP_CHUNK), 1)
    keep = j_i <= i_i
    g_rows = []
    for g in range(B_GROUPS):
        rows = slice(g * HEAD_DIM, (g + 1) * HEAD_DIM)
        xg = vg_t[rows]
        xg = xg - jnp.mean(xg, axis=0, keepdims=True)
        var = jnp.mean(xg * xg, axis=0, keepdims=True)
        vn = (xg * lax.rsqrt(var + EPS) * lng_ref[rows] + lnb_ref[rows]).astype(BF16)
        stacked = jnp.concatenate(
            [vn[:, r * GMLP_CHUNK:(r + 1) * GMLP_CHUNK] for r in range(n_chunks)], axis=0)
        w_t = jnp.where(keep, wst_ref[g], 0.0).astype(BF16)
        mixed = _dot(stacked, w_t) + bs_ref[g:g + 1, :]
        mixed = jnp.concatenate(
            [mixed[r * HEAD_DIM:(r + 1) * HEAD_DIM] for r in range(n_chunks)], axis=1)
        g_rows.append(u_t[rows] * mixed)
    g_out[...] = jnp.concatenate(g_rows, axis=0).T.astype(g_out.dtype)


def _ev_params(norm_mix, w_in, q_gain, k_gain, ln_g, ln_b, w_s, b_s):
    n_ev = w_in.shape[0]
    tm = ROW_TILE
    along_lanes = lambda a: jnp.broadcast_to(a[:, :, None], a.shape + (tm,))
    gn = norm_mix[0::2].reshape(n_ev, 1, D_MODEL)
    w_t = jnp.swapaxes(w_in.astype(BF16), 1, 2)
    w_k = w_in[:, :, A_WIDTH:2 * A_WIDTH].astype(BF16)
    qg = along_lanes(q_gain)
    kg = jnp.tile(k_gain, (1, LANES // HEAD_DIM)).reshape(n_ev, 1, LANES)
    lng = along_lanes(ln_g.reshape(n_ev, B_WIDTH))
    lnb = along_lanes(ln_b.reshape(n_ev, B_WIDTH))
    w_s_t = jnp.swapaxes(w_s, 2, 3)
    return (gn, w_t, w_k, qg, kg), (lng, lnb, w_s_t, b_s)


def _ev_proj(x, params, layer, cos_t, sin_t):
    t = x.shape[0]
    tm = ROW_TILE
    row_spec = lambda n: pl.BlockSpec((tm, n), lambda i: (i, 0))
    col_spec = pl.BlockSpec((A_WIDTH, tm), lambda i: (0, i))
    head, tail = params
    return pl.pallas_call(
        _ev_proj_kernel,
        grid=(t // tm,),
        in_specs=([row_spec(D_MODEL)] + [_layer_spec(a, layer) for a in head]
                  + [row_spec(LANES), row_spec(LANES)] + [_layer_spec(a, layer) for a in tail]),
        out_specs=[col_spec, row_spec(A_WIDTH), col_spec, row_spec(B_WIDTH)],
        out_shape=[jax.ShapeDtypeStruct((A_WIDTH, t), BF16),
                   jax.ShapeDtypeStruct((t, A_WIDTH), BF16),
                   jax.ShapeDtypeStruct((A_WIDTH, t), BF16),
                   jax.ShapeDtypeStruct((t, B_WIDTH), BF16)],
        compiler_params=_params(1),
        name="ev_proj",
    )(x, *head, cos_t, sin_t, *tail)


def _moba_kernel(qt_ref, k_ref, vt_ref, o_ref, s_ref):
    seq = k_ref.shape[0]
    nb = seq // MOBA_BLOCK
    blk = MOBA_BLOCK
    scale = HEAD_DIM ** -0.5

    k_mean = jnp.mean(k_ref[...].astype(F32).reshape(nb, blk, LANES), axis=1)
    k_mean = jnp.concatenate([k_mean, jnp.zeros((16 - nb, LANES), F32)], axis=0).astype(BF16)

    lo = lax.broadcasted_iota(jnp.int32, (LANES, blk), 0) < HEAD_DIM
    key_i = lax.broadcasted_iota(jnp.int32, (blk, blk), 0)
    qry_i = lax.broadcasted_iota(jnp.int32, (blk, blk), 1)
    causal = key_i <= qry_i

    def scores(i, hd, slot):
        q_i = qt_ref[:, i * blk:(i + 1) * blk]
        head_mask = lo if hd == 0 else jnp.logical_not(lo)
        qh = jnp.where(head_mask, q_i, jnp.zeros_like(q_i)) * scale
        col_max = []
        for j in range(i + 1):
            sj = _dot(k_ref[j * blk:(j + 1) * blk, :], qh)
            if j == i:
                sj = jnp.where(causal, sj, NEG_INF)
            s_ref[slot, j * blk:(j + 1) * blk, :] = sj
            col_max.append(jnp.max(sj, axis=0, keepdims=True))
            yield
        if i > MOBA_TOPK:
            gate = _dot(k_mean, qh)
            g = [gate[j:j + 1, :] for j in range(i)]
            bias = []
            for j in range(i):
                cnt = jnp.zeros_like(g[j])
                for jp in range(i):
                    if jp == j:
                        continue
                    ahead = (g[jp] >= g[j]) if jp < j else (g[jp] > g[j])
                    cnt = cnt + jnp.where(ahead, 1.0, 0.0)
                bias.append(jnp.where(cnt < MOBA_TOPK, 0.0, NEG_INF))
        else:
            bias = [None] * i
        m = col_max[i]
        for j in range(i):
            m = jnp.maximum(m, col_max[j] if bias[j] is None else col_max[j] + bias[j])
        shifts = [-m if bias[j] is None else bias[j] - m for j in range(i)] + [-m]
        return slot, shifts

    def weighted_values(i, hd, slot, shifts):
        ones = jnp.ones((16, blk), BF16)
        acc = jnp.zeros((HEAD_DIM + 16, blk), F32)
        for j in range(i + 1):
            sj = s_ref[slot, j * blk:(j + 1) * blk, :]
            p = jnp.exp(sj + shifts[j]).astype(BF16)
            v_aug = jnp.concatenate(
                [vt_ref[hd * HEAD_DIM:(hd + 1) * HEAD_DIM, j * blk:(j + 1) * blk], ones], axis=0)
            acc = acc + _dot(v_aug, p)
            yield
        return acc[:HEAD_DIM] / acc[HEAD_DIM:HEAD_DIM + 1]

    def interleave(*stages):
        results = [None] * len(stages)
        live = list(enumerate(stages))
        while live:
            for idx, stage in list(live):
                try:
                    next(stage)
                except StopIteration as done:
                    results[idx] = done.value
                    live.remove((idx, stage))
        return results

    groups = [(i, hd) for i in range(nb) for hd in range(LANES // HEAD_DIM)]
    n_slots = s_ref.shape[0]
    pending = interleave(*[scores(*g, n % n_slots)
                           for n, g in enumerate(groups[:MOBA_LOOKAHEAD])])
    head_out = []
    for n, (i, hd) in enumerate(groups):
        stages = [weighted_values(i, hd, *pending.pop(0))]
        if n + MOBA_LOOKAHEAD < len(groups):
            ahead = n + MOBA_LOOKAHEAD
            stages.append(scores(*groups[ahead], ahead % n_slots))
        done = interleave(*stages)
        head_out.append(done[0])
        if len(done) > 1:
            pending.append(done[1])
        if hd == LANES // HEAD_DIM - 1:
            o_t = jnp.concatenate(head_out, axis=0)
            o_ref[i * blk:(i + 1) * blk, :] = o_t.T.astype(o_ref.dtype)
            head_out = []


def _moba(q_t, k, v_t, batch, seq):
    n_pairs = A_WIDTH // LANES
    row_major = pl.BlockSpec((seq, LANES), lambda b, p: (b, p))
    feat_major = pl.BlockSpec((LANES, seq), lambda b, p: (p, b))
    return pl.pallas_call(
        _moba_kernel,
        grid=(batch, n_pairs),
        in_specs=[feat_major, row_major, feat_major],
        out_specs=row_major,
        out_shape=jax.ShapeDtypeStruct(k.shape, BF16),
        scratch_shapes=[pltpu.VMEM((MOBA_LOOKAHEAD + 1, seq, MOBA_BLOCK), F32)],
        compiler_params=_params(2),
        name="moba_attn",
    )(q_t, k, v_t)


def _od_mixer_kernel(x_ref, gn_ref, w_ref, wg1_ref, wg2_ref, bg_ref, og_ref, wo_ref,
                     o_ref, state_ref):
    @pl.when(pl.program_id(1) == 0)
    def _():
        state_ref[...] = jnp.zeros_like(state_ref)

    tm = x_ref.shape[0]
    c = GLA_CHUNK
    nc = tm // c
    x = x_ref[...]
    h = _rms_rows(x, gn_ref[...]).astype(BF16)
    q = _dot(h, w_ref[:, :GLA_QK_W]) * (GLA_DK ** -0.5)
    k = _dot(h, w_ref[:, GLA_QK_W:2 * GLA_QK_W])
    z = _dot(_dot(h, wg1_ref[...]).astype(BF16), wg2_ref[...]) + bg_ref[...]
    log_a = (jnp.minimum(z, 0.0) - jnp.log(1.0 + jnp.exp(-jnp.abs(z)))) * (1.0 / GLA_TAU)

    in_chunk = lax.broadcasted_iota(jnp.int32, (tm, GLA_QK_W), 0) % c
    b = log_a
    step = 1
    while step < c:
        b = b + jnp.where(in_chunk >= step, pltpu.roll(b, step, 0), 0.0)
        step *= 2
    ab = GLA_ATT_BLOCK
    t_i = lax.broadcasted_iota(jnp.int32, (ab, ab), 0)
    s_i = lax.broadcasted_iota(jnp.int32, (ab, ab), 1)
    causal = jnp.logical_and(t_i // c == s_i // c, s_i <= t_i)
    b3 = b.reshape(nc, c, GLA_QK_W)
    b_last = b3[:, c - 1:c, :]
    q_in = (q * jnp.exp(b)).astype(BF16)
    k_in = (k * jnp.exp(-b)).astype(BF16)
    k_state = (k.reshape(nc, c, GLA_QK_W) * jnp.exp(b_last - b3)).reshape(tm, GLA_QK_W)
    k_state = k_state.astype(BF16)
    decay = jnp.exp(b_last)

    def intra(hd):
        ks = slice(hd * GLA_DK, (hd + 1) * GLA_DK)
        v0 = 2 * GLA_QK_W + hd * GLA_DV
        r0 = 2 * GLA_QK_W + GLA_V_W + hd * GLA_DV
        v_h = _dot(h, w_ref[:, v0:v0 + GLA_DV]).astype(BF16)
        r_h = _dot(h, w_ref[:, r0:r0 + GLA_DV])
        o_intra = []
        for a0 in range(0, tm, ab):
            rows = slice(a0, a0 + ab)
            att = jnp.where(causal, _dot_nt(q_in[rows, ks], k_in[rows, ks]), 0.0)
            o_intra.append(_dot(att.astype(BF16), v_h[rows]))
        return v_h, r_h, jnp.concatenate(o_intra, axis=0)

    def recurrent(hd, v_h, r_h, o_intra):
        ks = slice(hd * GLA_DK, (hd + 1) * GLA_DK)
        state = state_ref[hd]
        parts = []
        for n in range(nc):
            rs = slice(n * c, (n + 1) * c)
            parts.append(o_intra[rs] + _dot_nt(q_in[rs, ks], state.astype(BF16)))
            state = decay[n][:, ks] * state + _dot_tn(v_h[rs], k_state[rs, ks])
        state_ref[hd] = state
        o = jnp.concatenate(parts, axis=0)
        o = _rms_rows(o, og_ref[:, hd * GLA_DV:(hd + 1) * GLA_DV])
        return (o * _silu(r_h)).astype(BF16)

    heads = [recurrent(hd, *intra(hd)) for hd in range(GLA_HEADS)]
    o_ref[...] = x + _dot(jnp.concatenate(heads, axis=-1), wo_ref[...])


def _od_params(norm_mix, w_in, w_g1, w_g2, b_g, o_gain, w_out):
    n_od = w_in.shape[0]
    wg1 = jnp.pad(w_g1, ((0, 0), (0, 0), (0, LANES - GLA_LOWRANK))).astype(BF16)
    wg2 = jnp.pad(w_g2, ((0, 0), (0, LANES - GLA_LOWRANK), (0, 0))).astype(BF16)
    return (norm_mix[1::2].reshape(n_od, 1, D_MODEL), w_in.astype(BF16), wg1, wg2,
            b_g.reshape(n_od, 1, GLA_QK_W), o_gain.reshape(n_od, 1, GLA_V_W),
            w_out.astype(BF16))


def _od_mixer(x, params, layer, batch, seq):
    tm = ROW_TILE
    steps = seq // tm
    row_spec = pl.BlockSpec((tm, D_MODEL), lambda b, i: (b * steps + i, 0))
    return pl.pallas_call(
        _od_mixer_kernel,
        grid=(batch, steps),
        in_specs=[row_spec] + [_layer_spec(a, layer) for a in params],
        out_specs=row_spec,
        out_shape=jax.ShapeDtypeStruct(x.shape, F32),
        scratch_shapes=[pltpu.VMEM((GLA_HEADS, GLA_DV, GLA_DK), F32)],
        compiler_params=_params(2),
        name="gla_mixer",
    )(x, *params)


def _mem_kv_kernel(m_ref, gn_ref, w_ref, kg_ref, k_out, v_out):
    h = _rms_rows(m_ref[...], gn_ref[...]).astype(BF16)
    k = _dot(h, w_ref[:, :D_MODEL])
    for hh in range(X_HEADS):
        sl = slice(hh * X_HEAD_DIM, (hh + 1) * X_HEAD_DIM)
        k_out[:, sl] = _rms_rows(k[:, sl], kg_ref[...]).astype(k_out.dtype)
    v_out[...] = _dot(h, w_ref[:, D_MODEL:]).astype(v_out.dtype)


def _mem_kv(mem, gn, w_kv, k_gain):
    t = mem.shape[0]
    depth = w_kv.shape[0]
    tm = ROW_TILE
    per_layer = lambda a: pl.BlockSpec((None,) + a.shape[1:], lambda l, i: (l, 0, 0))
    out_spec = pl.BlockSpec((None, tm, D_MODEL), lambda l, i: (l, i, 0))
    return pl.pallas_call(
        _mem_kv_kernel,
        grid=(depth, t // tm),
        in_specs=[pl.BlockSpec((tm, D_MODEL), lambda l, i: (i, 0)),
                  per_layer(gn), per_layer(w_kv), per_layer(k_gain)],
        out_specs=[out_spec, out_spec],
        out_shape=[jax.ShapeDtypeStruct((depth, t, D_MODEL), BF16)] * 2,
        compiler_params=_params(2),
        name="mem_kv",
    )(mem, gn, w_kv, k_gain)


def _xattn_ffn_body(x, gq_ref, wq_ref, qg_ref, k_ref, v_ref, wo_ref,
                    gf_ref, wgu_ref, wd_ref, o_ref):
    h = _rms_rows(x, gq_ref[...]).astype(BF16)
    q = _dot(h, wq_ref[...])
    scale = X_HEAD_DIM ** -0.5
    heads = []
    for hh in range(X_HEADS):
        sl = slice(hh * X_HEAD_DIM, (hh + 1) * X_HEAD_DIM)
        qh = _rms_rows(q[:, sl], qg_ref[...]).astype(BF16)
        s = _dot_nt(qh, k_ref[:, sl]) * scale
        e = jnp.exp(s - jnp.max(s, axis=-1, keepdims=True))
        p = (e / jnp.sum(e, axis=-1, keepdims=True)).astype(BF16)
        heads.append(_dot(p, v_ref[:, sl]).astype(BF16))
    x = x + _dot(jnp.concatenate(heads, axis=-1), wo_ref[...])

    h = _rms_rows(x, gf_ref[...]).astype(BF16)
    acc = x
    for lo, hi in FFN_SPLITS:
        g = _dot(h, wgu_ref[:, lo:hi])
        u = _dot(h, wgu_ref[:, D_FF + lo:D_FF + hi])
        acc = acc + _dot((_silu(g) * u).astype(BF16), wd_ref[lo:hi, :])
    o_ref[...] = acc


POST_WEIGHTS = ("w_q", "w_xo", "w_gu", "w_down")
N_POST_OPERANDS = 9


def _post_kernel(*refs, has_proj, n_cast):
    refs = list(refs)
    x = refs.pop(0)[...]
    if has_proj:
        a_ref, g_ref, wm_ref = refs[:3]
        del refs[:3]
        n_a = a_ref.shape[1]
        x = x + _dot(a_ref[...], wm_ref[:n_a, :]) + _dot(g_ref[...], wm_ref[n_a:, :])
    operands = refs[:N_POST_OPERANDS]
    cast_src = refs[N_POST_OPERANDS:N_POST_OPERANDS + n_cast]
    o_ref = refs[N_POST_OPERANDS + n_cast]
    cast_dst = refs[N_POST_OPERANDS + n_cast + 1:]
    for src, dst in zip(cast_src, cast_dst):
        dst[...] = src[...].astype(dst.dtype)
    _xattn_ffn_body(x, *operands, o_ref)


def _resident_spec(arr):
    nd = arr.ndim
    return pl.BlockSpec(arr.shape, lambda *_: (0,) * nd, pipeline_mode=pl.Buffered(1))


def _xattn_ffn(x, mixer, p, w, nxt, layer, k_mem, v_mem, batch, seq):
    tm = ROW_TILE
    steps = seq // tm
    n_steps = batch * steps
    flat = lambda b, i: b * steps + i
    row_spec = lambda n: pl.BlockSpec((tm, n), lambda b, i: (flat(b, i), 0))
    mem_spec = pl.BlockSpec((None, N_MEM, D_MODEL), lambda b, i: (layer, b, 0))
    ls = lambda name: _layer_spec(p[name], layer)
    specs = [row_spec(D_MODEL)]
    args = [x]
    if mixer is not None:
        a, g, w_m, m_layer = mixer
        specs += [row_spec(a.shape[1]), row_spec(g.shape[1]), _layer_spec(w_m, m_layer)]
        args += [a, g, w_m]
    specs += [ls("gq"), _resident_spec(w["w_q"]), ls("q_gain"), mem_spec, mem_spec,
              _resident_spec(w["w_xo"]), ls("gf"), _resident_spec(w["w_gu"]),
              _resident_spec(w["w_down"])]
    args += [p["gq"], w["w_q"], p["q_gain"], k_mem, v_mem, w["w_xo"], p["gf"], w["w_gu"],
             w["w_down"]]
    out_specs = [row_spec(D_MODEL)]
    out_shape = [jax.ShapeDtypeStruct(x.shape, F32)]
    n_cast = 0
    if nxt is not None:
        for name in POST_WEIGHTS:
            _, rows, cols = nxt[name].shape
            visits = 1 if rows % (16 * n_steps) == 0 else 2
            band = rows * visits // n_steps
            assert band % 16 == 0 and band * n_steps == rows * visits
            specs.append(pl.BlockSpec((None, band, cols),
                                      lambda b, i, v=visits: (layer + 1, flat(b, i) // v, 0)))
            args.append(nxt[name])
            out_specs.append(pl.BlockSpec((band, cols),
                                          lambda b, i, v=visits: (flat(b, i) // v, 0)))
            out_shape.append(jax.ShapeDtypeStruct((rows, cols), BF16))
        n_cast = len(POST_WEIGHTS)
    outs = pl.pallas_call(
        partial(_post_kernel, has_proj=mixer is not None, n_cast=n_cast),
        grid=(batch, steps),
        in_specs=specs,
        out_specs=out_specs,
        out_shape=out_shape,
        compiler_params=_params(2),
        name="xattn_ffn" if mixer is None else "proj_xattn_ffn",
    )(*args)
    nxt_w = dict(zip(POST_WEIGHTS, outs[1:])) if nxt is not None else None
    return outs[0], nxt_w


def kernel(x, mem, positions, norm_mix, norm_mem_q, norm_mem_kv, norm_ffn, ev_w_in, ev_q_gain, ev_k_gain, ev_w_s, ev_b_s, ev_ln_g, ev_ln_b, ev_w_out, od_w_in, od_w_g1, od_w_g2, od_b_g, od_o_gain, od_w_out, xa_w_q, xa_w_kv, xa_q_gain, xa_k_gain, xa_w_out, ffn_w_gu, ffn_w_down):
    batch, seq, d_model = x.shape
    assert d_model == D_MODEL and seq % ROW_TILE == 0 and seq % MOBA_BLOCK == 0
    assert mem.shape == (batch, N_MEM, D_MODEL)
    xf = x.reshape(batch * seq, D_MODEL)
    memf = mem.reshape(batch * N_MEM, D_MODEL)
    cos_t, sin_t = _rope_tables(positions)
    vec = lambda a: a.reshape(a.shape[0], 1, a.shape[1])
    ev = _ev_params(norm_mix, ev_w_in, ev_q_gain, ev_k_gain, ev_ln_g, ev_ln_b, ev_w_s, ev_b_s)
    ev_w_out_bf = ev_w_out.astype(BF16)
    od = _od_params(norm_mix, od_w_in, od_w_g1, od_w_g2, od_b_g, od_o_gain, od_w_out)
    post = dict(gq=vec(norm_mem_q), q_gain=vec(xa_q_gain), gf=vec(norm_ffn))
    post_f32 = dict(w_q=xa_w_q, w_xo=xa_w_out, w_gu=ffn_w_gu, w_down=ffn_w_down)
    post_w = {name: a[0].astype(BF16) for name, a in post_f32.items()}
    k_mem, v_mem = _mem_kv(memf, vec(norm_mem_kv), xa_w_kv.astype(BF16), vec(xa_k_gain))
    for l in range(DEPTH):
        i = l // 2
        if l % 2 == 0:
            q_t, k, v_t, g = _ev_proj(xf, ev, i, cos_t, sin_t)
            mixer = (_moba(q_t, k, v_t, batch, seq), g, ev_w_out_bf, i)
        else:
            xf = _od_mixer(xf, od, i, batch, seq)
            mixer = None
        xf, post_w = _xattn_ffn(xf, mixer, post, post_w, post_f32 if l + 1 < DEPTH else None,
                                l, k_mem, v_mem, batch, seq)
    return xf.reshape(batch, seq, D_MODEL)
```

```python
import math
from functools import partial

import jax
import jax.numpy as jnp
from jax import lax
from jax.experimental import pallas as pl
from jax.experimental.pallas import tpu as pltpu

F32 = jnp.float32
BF16 = jnp.bfloat16

D_MODEL = 1024
DEPTH = 4
N_MEM = 256
EPS = 1e-6
NEG_INF = -1e30

HEAD_DIM = 64
ROPE_DIM = HEAD_DIM // 4
ROPE_THETA = 500000.0

A_HEADS = 8
A_WIDTH = A_HEADS * HEAD_DIM
MOBA_BLOCK = 256
MOBA_TOPK = 3
MOBA_LOOKAHEAD = 3

B_GROUPS = 8
B_WIDTH = B_GROUPS * HEAD_DIM
GMLP_CHUNK = 128

GLA_HEADS = 4
GLA_DK = (D_MODEL // 2) // GLA_HEADS
GLA_DV = D_MODEL // GLA_HEADS
GLA_QK_W = GLA_HEADS * GLA_DK
GLA_V_W = GLA_HEADS * GLA_DV
GLA_LOWRANK = 16
GLA_TAU = 16.0
GLA_CHUNK = 64
GLA_BATCH_PAIR = 2
GLA_ATT_BLOCK = 256

X_HEADS = 4
X_HEAD_DIM = D_MODEL // X_HEADS

D_FF = -(-8 * D_MODEL // (3 * 256)) * 256

LANES = 128
VMEM_LIMIT_BYTES = 56 * 1024 * 1024

ROW_TILE = 512
EV_TILES_PER_STEP = 4
FFN_SPLITS = tuple((lo, lo + 256) for lo in range(0, D_FF, 256))


def _params(n_axes):
    return pltpu.CompilerParams(
        dimension_semantics=("arbitrary",) * n_axes,
        vmem_limit_bytes=VMEM_LIMIT_BYTES)


def _layer_spec(stacked, layer):
    nd = stacked.ndim - 1
    return pl.BlockSpec((None,) + stacked.shape[1:], lambda *_: (layer,) + (0,) * nd,
                        pipeline_mode=pl.Buffered(1))


def _dot(a, b):
    return jnp.dot(a, b, preferred_element_type=F32)


def _dot_nt(a, b):
    return lax.dot_general(a, b, (((1,), (1,)), ((), ())),
                           preferred_element_type=F32)


def _dot_tn(a, b):
    return lax.dot_general(a, b, (((0,), (0,)), ((), ())),
                           preferred_element_type=F32)


def _rms_rows(x, gain):
    ms = jnp.mean(x * x, axis=-1, keepdims=True)
    return x * lax.rsqrt(ms + EPS) * gain


def _half_lane_mean(x, lo):
    s_lo = jnp.sum(jnp.where(lo, x, 0.0), axis=-1, keepdims=True)
    s_hi = jnp.sum(jnp.where(lo, 0.0, x), axis=-1, keepdims=True)
    return jnp.where(lo, s_lo, s_hi) * (1.0 / HEAD_DIM)


def _gelu(x):
    return 0.5 * x * (1.0 + lax.erf(x * math.sqrt(0.5)))


def _silu(x):
    return x / (1.0 + jnp.exp(-x))


def _rope_table_kernel(pos_ref, invf_ref, sign_ref, cos_ref, sin_ref):
    ang = pos_ref[...] * invf_ref[...]
    cos_ref[...] = jnp.cos(ang)
    sin_ref[...] = jnp.sin(ang) * sign_ref[...]


def _rope_tables(positions):
    t = positions.size
    half = ROPE_DIM // 2
    per_row = LANES // ROPE_DIM
    inv_freq = ROPE_THETA ** (-jnp.arange(half, dtype=F32) * 2.0 / ROPE_DIM)
    d = jnp.arange(LANES) % ROPE_DIM
    invf = inv_freq[d % half][None, :]
    sign = jnp.where(d < half, -1.0, 1.0).astype(F32)[None, :]
    pos = jnp.repeat(positions.astype(F32).reshape(t // per_row, per_row), ROPE_DIM, axis=1)
    full = pl.BlockSpec((t // per_row, LANES), lambda: (0, 0))
    row = pl.BlockSpec((1, LANES), lambda: (0, 0))
    cos_p, sin_p = pl.pallas_call(
        _rope_table_kernel,
        in_specs=[full, row, row],
        out_specs=[full, full],
        out_shape=[jax.ShapeDtypeStruct((t // per_row, LANES), F32)] * 2,
        name="rope_tables",
    )(pos, invf, sign)

    def per_token(packed, fill):
        rot = packed.reshape(t, ROPE_DIM)
        rest = jnp.full((t, HEAD_DIM - ROPE_DIM), fill, F32)
        return jnp.tile(jnp.concatenate([rot, rest], axis=1), (1, LANES // HEAD_DIM))

    return per_token(cos_p, 1.0), per_token(sin_p, 0.0)


def _alternate(chains):
    next(chains[0])
    live = list(chains)
    while live:
        for chain in list(live[::-1]):
            try:
                next(chain)
            except StopIteration:
                live.remove(chain)


def _ev_proj_kernel(x_ref, *refs):
    _alternate([_ev_proj_chain(e, x_ref, *refs) for e in range(x_ref.shape[0] // ROW_TILE)])


def _ev_proj_chain(e, x_ref, gn_ref, wt_ref, wk_ref, qg_ref, kg_ref, cos_ref, sin_ref,
                   lng_ref, lnb_ref, wst_ref, bs_ref,
                   qt_out, k_out, vt_out, g_out):
    tm = ROW_TILE
    tile = slice(e * tm, (e + 1) * tm)
    h = _rms_rows(x_ref[tile, :], gn_ref[...]).astype(BF16)
    cos = cos_ref[tile, :]
    sin = sin_ref[tile, :]
    half = ROPE_DIM // 2

    def section_t(row0, width):
        return _dot_nt(wt_ref[row0:row0 + width, :], h)

    cos_t = cos.T[:half]
    sin_t = sin.T[half:ROPE_DIM]
    q_t = section_t(0, A_WIDTH)
    for hd in range(A_HEADS):
        xq = q_t[hd * HEAD_DIM:(hd + 1) * HEAD_DIM]
        ms = jnp.mean(xq * xq, axis=0, keepdims=True)
        y = xq * lax.rsqrt(ms + EPS) * qg_ref[...]
        x1, x2 = y[:half], y[half:ROPE_DIM]
        y = jnp.concatenate([x1 * cos_t - x2 * sin_t, x2 * cos_t + x1 * sin_t, y[ROPE_DIM:]],
                            axis=0)
        qt_out[hd * HEAD_DIM:(hd + 1) * HEAD_DIM, tile] = y.astype(qt_out.dtype)
    yield

    lane = lax.broadcasted_iota(jnp.int32, (1, LANES), 1)
    lo = lane < HEAD_DIM
    first = (lane % HEAD_DIM) < half
    p = _dot(h, wk_ref[...])
    for c in range(A_WIDTH // LANES):
        xc = p[:, c * LANES:(c + 1) * LANES]
        y = xc * lax.rsqrt(_half_lane_mean(xc * xc, lo) + EPS) * kg_ref[...]
        partner = jnp.where(first, pltpu.roll(y, LANES - half, 1), pltpu.roll(y, half, 1))
        k_out[tile, c * LANES:(c + 1) * LANES] = (y * cos + partner * sin).astype(k_out.dtype)
    yield

    vt_out[:, tile] = section_t(2 * A_WIDTH, A_WIDTH).astype(vt_out.dtype)

    u_t = _gelu(section_t(3 * A_WIDTH, B_WIDTH))
    vg_t = _gelu(section_t(3 * A_WIDTH + B_WIDTH, B_WIDTH))
    yield
    n_chunks = tm // GMLP_CHUNK
    j_i = lax.broadcasted_iota(jnp.int32, (GMLP_CHUNK, GMLP_CHUNK), 0)
    i_i = lax.broadcasted_iota(jnp.int32, (GMLP_CHUNK, GMLP_CHUNK), 1)
    keep = j_i <= i_i
    g_rows = []
    for g in range(B_GROUPS):
        rows = slice(g * HEAD_DIM, (g + 1) * HEAD_DIM)
        xg = vg_t[rows]
        xg = xg - jnp.mean(xg, axis=0, keepdims=True)
        var = jnp.mean(xg * xg, axis=0, keepdims=True)
        vn = (xg * lax.rsqrt(var + EPS) * lng_ref[rows] + lnb_ref[rows]).astype(BF16)
        stacked = jnp.concatenate(
            [vn[:, r * GMLP_CHUNK:(r + 1) * GMLP_CHUNK] for r in range(n_chunks)], axis=0)
        w_t = jnp.where(keep, wst_ref[g], 0.0).astype(BF16)
        mixed = _dot(stacked, w_t) + bs_ref[g:g + 1, :]
        mixed = jnp.concatenate(
            [mixed[r * HEAD_DIM:(r + 1) * HEAD_DIM] for r in range(n_chunks)], axis=1)
        g_rows.append(u_t[rows] * mixed)
        if g == B_GROUPS // 2 - 1:
            yield
    g_out[tile, :] = jnp.concatenate(g_rows, axis=0).T.astype(g_out.dtype)


def _ev_params(norm_mix, w_in, q_gain, k_gain, ln_g, ln_b, w_s, b_s):
    n_ev = w_in.shape[0]
    tm = ROW_TILE
    along_lanes = lambda a: jnp.broadcast_to(a[:, :, None], a.shape + (tm,))
    gn = norm_mix[0::2].reshape(n_ev, 1, D_MODEL)
    w_t = jnp.swapaxes(w_in.astype(BF16), 1, 2)
    w_k = w_in[:, :, A_WIDTH:2 * A_WIDTH].astype(BF16)
    qg = along_lanes(q_gain)
    kg = jnp.tile(k_gain, (1, LANES // HEAD_DIM)).reshape(n_ev, 1, LANES)
    lng = along_lanes(ln_g.reshape(n_ev, B_WIDTH))
    lnb = along_lanes(ln_b.reshape(n_ev, B_WIDTH))
    w_s_t = jnp.swapaxes(w_s, 2, 3)
    return (gn, w_t, w_k, qg, kg), (lng, lnb, w_s_t, b_s)


def _ev_proj(x, params, layer, cos_t, sin_t):
    t = x.shape[0]
    tm = ROW_TILE * EV_TILES_PER_STEP
    row_spec = lambda n: pl.BlockSpec((tm, n), lambda i: (i, 0))
    col_spec = pl.BlockSpec((A_WIDTH, tm), lambda i: (0, i))
    head, tail = params
    return pl.pallas_call(
        _ev_proj_kernel,
        grid=(t // tm,),
        in_specs=([row_spec(D_MODEL)] + [_layer_spec(a, layer) for a in head]
                  + [row_spec(LANES), row_spec(LANES)] + [_layer_spec(a, layer) for a in tail]),
        out_specs=[col_spec, row_spec(A_WIDTH), col_spec, row_spec(B_WIDTH)],
        out_shape=[jax.ShapeDtypeStruct((A_WIDTH, t), BF16),
                   jax.ShapeDtypeStruct((t, A_WIDTH), BF16),
                   jax.ShapeDtypeStruct((A_WIDTH, t), BF16),
                   jax.ShapeDtypeStruct((t, B_WIDTH), BF16)],
        compiler_params=_params(1),
        name="ev_proj",
    )(x, *head, cos_t, sin_t, *tail)


def _moba_kernel(qt_ref, k_ref, vt_ref, o_ref, s_ref):
    seq = k_ref.shape[0]
    nb = seq // MOBA_BLOCK
    blk = MOBA_BLOCK
    scale = HEAD_DIM ** -0.5

    k_mean = jnp.mean(k_ref[...].astype(F32).reshape(nb, blk, LANES), axis=1)
    k_mean = jnp.concatenate([k_mean, jnp.zeros((16 - nb, LANES), F32)], axis=0).astype(BF16)

    lo = lax.broadcasted_iota(jnp.int32, (LANES, blk), 0) < HEAD_DIM
    key_i = lax.broadcasted_iota(jnp.int32, (blk, blk), 0)
    qry_i = lax.broadcasted_iota(jnp.int32, (blk, blk), 1)
    causal = key_i <= qry_i

    def scores(i, hd, slot):
        q_i = qt_ref[:, i * blk:(i + 1) * blk]
        head_mask = lo if hd == 0 else jnp.logical_not(lo)
        qh = jnp.where(head_mask, q_i, jnp.zeros_like(q_i)) * scale
        col_max = []
        for j in range(i + 1):
            sj = _dot(k_ref[j * blk:(j + 1) * blk, :], qh)
            if j == i:
                sj = jnp.where(causal, sj, NEG_INF)
            s_ref[slot, j * blk:(j + 1) * blk, :] = sj
            col_max.append(jnp.max(sj, axis=0, keepdims=True))
            yield
        if i > MOBA_TOPK:
            gate = _dot(k_mean, qh)
            g = [gate[j:j + 1, :] for j in range(i)]
            bias = []
            for j in range(i):
                cnt = jnp.zeros_like(g[j])
                for jp in range(i):
                    if jp == j:
                        continue
                    ahead = (g[jp] >= g[j]) if jp < j else (g[jp] > g[j])
                    cnt = cnt + jnp.where(ahead, 1.0, 0.0)
                bias.append(jnp.where(cnt < MOBA_TOPK, 0.0, NEG_INF))
        else:
            bias = [None] * i
        m = col_max[i]
        for j in range(i):
            m = jnp.maximum(m, col_max[j] if bias[j] is None else col_max[j] + bias[j])
        shifts = [-m if bias[j] is None else bias[j] - m for j in range(i)] + [-m]
        return slot, shifts

    def weighted_values(i, hd, slot, shifts):
        ones = jnp.ones((16, blk), BF16)
        acc = jnp.zeros((HEAD_DIM + 16, blk), F32)
        for j in range(i + 1):
            sj = s_ref[slot, j * blk:(j + 1) * blk, :]
            p = jnp.exp(sj + shifts[j]).astype(BF16)
            v_aug = jnp.concatenate(
                [vt_ref[hd * HEAD_DIM:(hd + 1) * HEAD_DIM, j * blk:(j + 1) * blk], ones], axis=0)
            acc = acc + _dot(v_aug, p)
            yield
        return acc[:HEAD_DIM] / acc[HEAD_DIM:HEAD_DIM + 1]

    def interleave(*stages):
        results = [None] * len(stages)
        live = list(enumerate(stages))
        while live:
            for idx, stage in list(live):
                try:
                    next(stage)
                except StopIteration as done:
                    results[idx] = done.value
                    live.remove((idx, stage))
        return results

    groups = [(i, hd) for i in range(nb) for hd in range(LANES // HEAD_DIM)]
    n_slots = s_ref.shape[0]
    pending = interleave(*[scores(*g, n % n_slots)
                           for n, g in enumerate(groups[:MOBA_LOOKAHEAD])])
    head_out = []
    for n, (i, hd) in enumerate(groups):
        stages = [weighted_values(i, hd, *pending.pop(0))]
        if n + MOBA_LOOKAHEAD < len(groups):
            ahead = n + MOBA_LOOKAHEAD
            stages.append(scores(*groups[ahead], ahead % n_slots))
        done = interleave(*stages)
        head_out.append(done[0])
        if len(done) > 1:
            pending.append(done[1])
        if hd == LANES // HEAD_DIM - 1:
            o_t = jnp.concatenate(head_out, axis=0)
            o_ref[i * blk:(i + 1) * blk, :] = o_t.T.astype(o_ref.dtype)
            head_out = []


def _moba(q_t, k, v_t, batch, seq):
    n_pairs = A_WIDTH // LANES
    row_major = pl.BlockSpec((seq, LANES), lambda b, p: (b, p))
    feat_major = pl.BlockSpec((LANES, seq), lambda b, p: (p, b))
    return pl.pallas_call(
        _moba_kernel,
        grid=(batch, n_pairs),
        in_specs=[feat_major, row_major, feat_major],
        out_specs=row_major,
        out_shape=jax.ShapeDtypeStruct(k.shape, BF16),
        scratch_shapes=[pltpu.VMEM((MOBA_LOOKAHEAD + 1, seq, MOBA_BLOCK), F32)],
        compiler_params=_params(2),
        name="moba_attn",
    )(q_t, k, v_t)


def _od_mixer_kernel(x_ref, gn_ref, w_ref, wg1_ref, wg2_ref, bg_ref, og_ref, wo_ref,
                     o_ref, state_ref):
    @pl.when(pl.program_id(1) == 0)
    def _():
        state_ref[...] = jnp.zeros_like(state_ref)

    _alternate([_od_mixer_chain(e, x_ref, gn_ref, w_ref, wg1_ref, wg2_ref, bg_ref, og_ref,
                                wo_ref, o_ref, state_ref) for e in range(x_ref.shape[0])])


def _od_mixer_chain(e, x_ref, gn_ref, w_ref, wg1_ref, wg2_ref, bg_ref, og_ref, wo_ref,
                    o_ref, state_ref):
    tm = x_ref.shape[1]
    c = GLA_CHUNK
    nc = tm // c
    x = x_ref[e]
    h = _rms_rows(x, gn_ref[...]).astype(BF16)
    q = _dot(h, w_ref[:, :GLA_QK_W]) * (GLA_DK ** -0.5)
    k = _dot(h, w_ref[:, GLA_QK_W:2 * GLA_QK_W])
    z = _dot(_dot(h, wg1_ref[...]).astype(BF16), wg2_ref[...]) + bg_ref[...]
    log_a = (jnp.minimum(z, 0.0) - jnp.log(1.0 + jnp.exp(-jnp.abs(z)))) * (1.0 / GLA_TAU)

    in_chunk = lax.broadcasted_iota(jnp.int32, (tm, GLA_QK_W), 0) % c
    b = log_a
    step = 1
    while step < c:
        b = b + jnp.where(in_chunk >= step, pltpu.roll(b, step, 0), 0.0)
        step *= 2
    ab = GLA_ATT_BLOCK
    t_i = lax.broadcasted_iota(jnp.int32, (ab, ab), 0)
    s_i = lax.broadcasted_iota(jnp.int32, (ab, ab), 1)
    causal = jnp.logical_and(t_i // c == s_i // c, s_i <= t_i)
    b3 = b.reshape(nc, c, GLA_QK_W)
    b_last = b3[:, c - 1:c, :]
    q_in = (q * jnp.exp(b)).astype(BF16)
    k_in = (k * jnp.exp(-b)).astype(BF16)
    k_state = (k.reshape(nc, c, GLA_QK_W) * jnp.exp(b_last - b3)).reshape(tm, GLA_QK_W)
    k_state = k_state.astype(BF16)
    decay = jnp.exp(b_last)
    yield

    def intra(hd):
        ks = slice(hd * GLA_DK, (hd + 1) * GLA_DK)
        v0 = 2 * GLA_QK_W + hd * GLA_DV
        r0 = 2 * GLA_QK_W + GLA_V_W + hd * GLA_DV
        v_h = _dot(h, w_ref[:, v0:v0 + GLA_DV]).astype(BF16)
        r_h = _dot(h, w_ref[:, r0:r0 + GLA_DV])
        o_intra = []
        for a0 in range(0, tm, ab):
            rows = slice(a0, a0 + ab)
            att = jnp.where(causal, _dot_nt(q_in[rows, ks], k_in[rows, ks]), 0.0)
            o_intra.append(_dot(att.astype(BF16), v_h[rows]))
        return v_h, r_h, jnp.concatenate(o_intra, axis=0)

    def recurrent(hd, v_h, r_h, o_intra):
        ks = slice(hd * GLA_DK, (hd + 1) * GLA_DK)
        state = state_ref[e, hd]
        parts = []
        for n in range(nc):
            rs = slice(n * c, (n + 1) * c)
            parts.append(o_intra[rs] + _dot_nt(q_in[rs, ks], state.astype(BF16)))
            state = decay[n][:, ks] * state + _dot_tn(v_h[rs], k_state[rs, ks])
        state_ref[e, hd] = state
        o = jnp.concatenate(parts, axis=0)
        o = _rms_rows(o, og_ref[:, hd * GLA_DV:(hd + 1) * GLA_DV])
        return (o * _silu(r_h)).astype(BF16)

    heads = []
    for hd in range(GLA_HEADS):
        pre = intra(hd)
        yield
        heads.append(recurrent(hd, *pre))
        yield
    o_ref[e] = x + _dot(jnp.concatenate(heads, axis=-1), wo_ref[...])


def _od_params(norm_mix, w_in, w_g1, w_g2, b_g, o_gain, w_out):
    n_od = w_in.shape[0]
    wg1 = jnp.pad(w_g1, ((0, 0), (0, 0), (0, LANES - GLA_LOWRANK))).astype(BF16)
    wg2 = jnp.pad(w_g2, ((0, 0), (0, LANES - GLA_LOWRANK), (0, 0))).astype(BF16)
    return (norm_mix[1::2].reshape(n_od, 1, D_MODEL), w_in.astype(BF16), wg1, wg2,
            b_g.reshape(n_od, 1, GLA_QK_W), o_gain.reshape(n_od, 1, GLA_V_W),
            w_out.astype(BF16))


def _od_mixer(x, params, layer, batch, seq):
    tm = ROW_TILE
    pair = GLA_BATCH_PAIR
    assert batch % pair == 0
    x3 = x.reshape(batch, seq, D_MODEL)
    tile_spec = pl.BlockSpec((pair, tm, D_MODEL), lambda b, i: (b, i, 0))
    out = pl.pallas_call(
        _od_mixer_kernel,
        grid=(batch // pair, seq // tm),
        in_specs=[tile_spec] + [_layer_spec(a, layer) for a in params],
        out_specs=tile_spec,
        out_shape=jax.ShapeDtypeStruct(x3.shape, F32),
        scratch_shapes=[pltpu.VMEM((pair, GLA_HEADS, GLA_DV, GLA_DK), F32)],
        compiler_params=_params(2),
        name="gla_mixer",
    )(x3, *params)
    return out.reshape(x.shape)


def _mem_kv_kernel(m_ref, gn_ref, w_ref, kg_ref, k_out, v_out):
    h = _rms_rows(m_ref[...], gn_ref[...]).astype(BF16)
    k = _dot(h, w_ref[:, :D_MODEL])
    for hh in range(X_HEADS):
        sl = slice(hh * X_HEAD_DIM, (hh + 1) * X_HEAD_DIM)
        k_out[:, sl] = _rms_rows(k[:, sl], kg_ref[...]).astype(k_out.dtype)
    v_out[...] = _dot(h, w_ref[:, D_MODEL:]).astype(v_out.dtype)


def _mem_kv(mem, gn, w_kv, k_gain):
    t = mem.shape[0]
    depth = w_kv.shape[0]
    tm = ROW_TILE
    per_layer = lambda a: pl.BlockSpec((None,) + a.shape[1:], lambda l, i: (l, 0, 0))
    out_spec = pl.BlockSpec((None, tm, D_MODEL), lambda l, i: (l, i, 0))
    return pl.pallas_call(
        _mem_kv_kernel,
        grid=(depth, t // tm),
        in_specs=[pl.BlockSpec((tm, D_MODEL), lambda l, i: (i, 0)),
                  per_layer(gn), per_layer(w_kv), per_layer(k_gain)],
        out_specs=[out_spec, out_spec],
        out_shape=[jax.ShapeDtypeStruct((depth, t, D_MODEL), BF16)] * 2,
        compiler_params=_params(2),
        name="mem_kv",
    )(mem, gn, w_kv, k_gain)


def _xattn_ffn_body(x, gq_ref, wq_ref, qg_ref, k_ref, v_ref, wo_ref,
                    gf_ref, wgu_ref, wd_ref, o_ref):
    h = _rms_rows(x, gq_ref[...]).astype(BF16)
    q = _dot(h, wq_ref[...])
    scale = X_HEAD_DIM ** -0.5
    heads = []
    for hh in range(X_HEADS):
        sl = slice(hh * X_HEAD_DIM, (hh + 1) * X_HEAD_DIM)
        qh = _rms_rows(q[:, sl], qg_ref[...]).astype(BF16)
        s = _dot_nt(qh, k_ref[:, sl]) * scale
        e = jnp.exp(s - jnp.max(s, axis=-1, keepdims=True))
        p = (e / jnp.sum(e, axis=-1, keepdims=True)).astype(BF16)
        heads.append(_dot(p, v_ref[:, sl]).astype(BF16))
    x = x + _dot(jnp.concatenate(heads, axis=-1), wo_ref[...])

    h = _rms_rows(x, gf_ref[...]).astype(BF16)
    acc = x
    for lo, hi in FFN_SPLITS:
        g = _dot(h, wgu_ref[:, lo:hi])
        u = _dot(h, wgu_ref[:, D_FF + lo:D_FF + hi])
        acc = acc + _dot((_silu(g) * u).astype(BF16), wd_ref[lo:hi, :])
    o_ref[...] = acc


POST_WEIGHTS = ("w_q", "w_xo", "w_gu", "w_down")
N_POST_OPERANDS = 9


def _post_kernel(*refs, has_proj, n_cast):
    refs = list(refs)
    x = refs.pop(0)[...]
    if has_proj:
        a_ref, g_ref, wm_ref = refs[:3]
        del refs[:3]
        n_a = a_ref.shape[1]
        x = x + _dot(a_ref[...], wm_ref[:n_a, :]) + _dot(g_ref[...], wm_ref[n_a:, :])
    operands = refs[:N_POST_OPERANDS]
    cast_src = refs[N_POST_OPERANDS:N_POST_OPERANDS + n_cast]
    o_ref = refs[N_POST_OPERANDS + n_cast]
    cast_dst = refs[N_POST_OPERANDS + n_cast + 1:]
    for src, dst in zip(cast_src, cast_dst):
        dst[...] = src[...].astype(dst.dtype)
    _xattn_ffn_body(x, *operands, o_ref)


def _resident_spec(arr):
    nd = arr.ndim
    return pl.BlockSpec(arr.shape, lambda *_: (0,) * nd, pipeline_mode=pl.Buffered(1))


def _xattn_ffn(x, mixer, p, w, nxt, layer, k_mem, v_mem, batch, seq):
    tm = ROW_TILE
    steps = seq // tm
    n_steps = batch * steps
    flat = lambda b, i: b * steps + i
    row_spec = lambda n: pl.BlockSpec((tm, n), lambda b, i: (flat(b, i), 0))
    mem_spec = pl.BlockSpec((None, N_MEM, D_MODEL), lambda b, i: (layer, b, 0))
    ls = lambda name: _layer_spec(p[name], layer)
    specs = [row_spec(D_MODEL)]
    args = [x]
    if mixer is not None:
        a, g, w_m, m_layer = mixer
        specs += [row_spec(a.shape[1]), row_spec(g.shape[1]), _layer_spec(w_m, m_layer)]
        args += [a, g, w_m]
    specs += [ls("gq"), _resident_spec(w["w_q"]), ls("q_gain"), mem_spec, mem_spec,
              _resident_spec(w["w_xo"]), ls("gf"), _resident_spec(w["w_gu"]),
              _resident_spec(w["w_down"])]
    args += [p["gq"], w["w_q"], p["q_gain"], k_mem, v_mem, w["w_xo"], p["gf"], w["w_gu"],
             w["w_down"]]
    out_specs = [row_spec(D_MODEL)]
    out_shape = [jax.ShapeDtypeStruct(x.shape, F32)]
    n_cast = 0
    if nxt is not None:
        for name in POST_WEIGHTS:
            _, rows, cols = nxt[name].shape
            visits = 1 if rows % (16 * n_steps) == 0 else 2
            band = rows * visits // n_steps
            assert band % 16 == 0 and band * n_steps == rows * visits
            specs.append(pl.BlockSpec((None, band, cols),
                                      lambda b, i, v=visits: (layer + 1, flat(b, i) // v, 0)))
            args.append(nxt[name])
            out_specs.append(pl.BlockSpec((band, cols),
                                          lambda b, i, v=visits: (flat(b, i) // v, 0)))
            out_shape.append(jax.ShapeDtypeStruct((rows, cols), BF16))
        n_cast = len(POST_WEIGHTS)
    outs = pl.pallas_call(
        partial(_post_kernel, has_proj=mixer is not None, n_cast=n_cast),
        grid=(batch, steps),
        in_specs=specs,
        out_specs=out_specs,
        out_shape=out_shape,
        compiler_params=_params(2),
        name="xattn_ffn" if mixer is None else "proj_xattn_ffn",
    )(*args)
    nxt_w = dict(zip(POST_WEIGHTS, outs[1:])) if nxt is not None else None
    return outs[0], nxt_w


def kernel(x, mem, positions, norm_mix, norm_mem_q, norm_mem_kv, norm_ffn, ev_w_in, ev_q_gain, ev_k_gain, ev_w_s, ev_b_s, ev_ln_g, ev_ln_b, ev_w_out, od_w_in, od_w_g1, od_w_g2, od_b_g, od_o_gain, od_w_out, xa_w_q, xa_w_kv, xa_q_gain, xa_k_gain, xa_w_out, ffn_w_gu, ffn_w_down):
    batch, seq, d_model = x.shape
    assert d_model == D_MODEL and seq % ROW_TILE == 0 and seq % MOBA_BLOCK == 0
    assert mem.shape == (batch, N_MEM, D_MODEL)
    xf = x.reshape(batch * seq, D_MODEL)
    memf = mem.reshape(batch * N_MEM, D_MODEL)
    cos_t, sin_t = _rope_tables(positions)
    vec = lambda a: a.reshape(a.shape[0], 1, a.shape[1])
    ev = _ev_params(norm_mix, ev_w_in, ev_q_gain, ev_k_gain, ev_ln_g, ev_ln_b, ev_w_s, ev_b_s)
    ev_w_out_bf = ev_w_out.astype(BF16)
    od = _od_params(norm_mix, od_w_in, od_w_g1, od_w_g2, od_b_g, od_o_gain, od_w_out)
    post = dict(gq=vec(norm_mem_q), q_gain=vec(xa_q_gain), gf=vec(norm_ffn))
    post_f32 = dict(w_q=xa_w_q, w_xo=xa_w_out, w_gu=ffn_w_gu, w_down=ffn_w_down)
    post_w = {name: a[0].astype(BF16) for name, a in post_f32.items()}
    k_mem, v_mem = _mem_kv(memf, vec(norm_mem_kv), xa_w_kv.astype(BF16), vec(xa_k_gain))
    for l in range(DEPTH):
        i = l // 2
        if l % 2 == 0:
            q_t, k, v_t, g = _ev_proj(xf, ev, i, cos_t, sin_t)
            mixer = (_moba(q_t, k, v_t, batch, seq), g, ev_w_out_bf, i)
        else:
            xf = _od_mixer(xf, od, i, batch, seq)
            mixer = None
        xf, post_w = _xattn_ffn(xf, mixer, post, post_w, post_f32 if l + 1 < DEPTH else None,
                                l, k_mem, v_mem, batch, seq)
    return xf.reshape(batch, seq, D_MODEL)
```

```python
import math
from functools import partial

import jax
import jax.numpy as jnp
from jax import lax
from jax.experimental import pallas as pl
from jax.experimental.pallas import tpu as pltpu

F32 = jnp.float32
BF16 = jnp.bfloat16

D_MODEL = 1024
DEPTH = 4
N_MEM = 256
EPS = 1e-6
NEG_INF = -1e30

HEAD_DIM = 64
ROPE_DIM = HEAD_DIM // 4
ROPE_THETA = 500000.0

A_HEADS = 8
A_WIDTH = A_HEADS * HEAD_DIM
MOBA_BLOCK = 256
MOBA_TOPK = 3
MOBA_LOOKAHEAD = 3

B_GROUPS = 8
B_WIDTH = B_GROUPS * HEAD_DIM
GMLP_CHUNK = 128

GLA_HEADS = 4
GLA_DK = (D_MODEL // 2) // GLA_HEADS
GLA_DV = D_MODEL // GLA_HEADS
GLA_QK_W = GLA_HEADS * GLA_DK
GLA_V_W = GLA_HEADS * GLA_DV
GLA_LOWRANK = 16
GLA_TAU = 16.0
GLA_CHUNK = 64
GLA_BATCH_PAIR = 2
GLA_ATT_BLOCK = 256

X_HEADS = 4
X_HEAD_DIM = D_MODEL // X_HEADS

D_FF = -(-8 * D_MODEL // (3 * 256)) * 256

LANES = 128
VMEM_LIMIT_BYTES = 56 * 1024 * 1024

ROW_TILE = 512
EV_TILES_PER_STEP = 4
FFN_SPLITS = tuple((lo, lo + 256) for lo in range(0, D_FF, 256))


def _params(n_axes):
    return pltpu.CompilerParams(
        dimension_semantics=("arbitrary",) * n_axes,
        vmem_limit_bytes=VMEM_LIMIT_BYTES)


def _layer_spec(stacked, layer):
    nd = stacked.ndim - 1
    return pl.BlockSpec((None,) + stacked.shape[1:], lambda *_: (layer,) + (0,) * nd,
                        pipeline_mode=pl.Buffered(1))


def _dot(a, b):
    return jnp.dot(a, b, preferred_element_type=F32)


def _dot_nt(a, b):
    return lax.dot_general(a, b, (((1,), (1,)), ((), ())),
                           preferred_element_type=F32)


def _dot_tn(a, b):
    return lax.dot_general(a, b, (((0,), (0,)), ((), ())),
                           preferred_element_type=F32)


def _rms_rows(x, gain):
    ms = jnp.mean(x * x, axis=-1, keepdims=True)
    return x * lax.rsqrt(ms + EPS) * gain


def _gelu(x):
    return 0.5 * x * (1.0 + lax.erf(x * math.sqrt(0.5)))


def _silu(x):
    return x / (1.0 + jnp.exp(-x))


def _rope_table_kernel(pos_ref, invf_ref, cos_ref, sin_ref):
    ang = pos_ref[...] * invf_ref[...]
    cos_ref[...] = jnp.cos(ang)
    sin_ref[...] = jnp.sin(ang)


def _rope_tables(positions):
    t = positions.size
    half = ROPE_DIM // 2
    inv_freq = ROPE_THETA ** (-jnp.arange(half, dtype=F32) * 2.0 / ROPE_DIM)
    invf = jnp.broadcast_to(inv_freq[:, None], (half, t))
    pos = positions.astype(F32).reshape(1, t)
    whole = lambda rows: pl.BlockSpec((rows, t), lambda: (0, 0))
    return pl.pallas_call(
        _rope_table_kernel,
        in_specs=[whole(1), whole(half)],
        out_specs=[whole(half), whole(half)],
        out_shape=[jax.ShapeDtypeStruct((half, t), F32)] * 2,
        name="rope_tables",
    )(pos, invf)


def _alternate(chains):
    next(chains[0])
    live = list(chains)
    while live:
        for chain in list(live[::-1]):
            try:
                next(chain)
            except StopIteration:
                live.remove(chain)


def _ev_proj_kernel(x_ref, *refs):
    _alternate([_ev_proj_chain(e, x_ref, *refs) for e in range(x_ref.shape[0] // ROW_TILE)])


def _ev_proj_chain(e, x_ref, gn_ref, wt_ref, qg_ref, kg_ref, cos_ref, sin_ref,
                   lng_ref, lnb_ref, wst_ref, bs_ref,
                   qt_out, k_out, vt_out, g_out):
    tm = ROW_TILE
    tile = slice(e * tm, (e + 1) * tm)
    h = _rms_rows(x_ref[tile, :], gn_ref[...]).astype(BF16)
    cos_t = cos_ref[:, tile]
    sin_t = sin_ref[:, tile]
    half = ROPE_DIM // 2

    def section_t(row0, width):
        return _dot_nt(wt_ref[row0:row0 + width, :], h)

    def normed_rotated_heads(row0, gain_ref):
        sec = section_t(row0, A_WIDTH)
        heads = []
        for hd in range(A_HEADS):
            xh = sec[hd * HEAD_DIM:(hd + 1) * HEAD_DIM]
            ms = jnp.mean(xh * xh, axis=0, keepdims=True)
            y = xh * lax.rsqrt(ms + EPS) * gain_ref[...]
            x1, x2 = y[:half], y[half:ROPE_DIM]
            heads.append(jnp.concatenate(
                [x1 * cos_t - x2 * sin_t, x2 * cos_t + x1 * sin_t, y[ROPE_DIM:]], axis=0))
        return heads

    for hd, y in enumerate(normed_rotated_heads(0, qg_ref)):
        qt_out[hd * HEAD_DIM:(hd + 1) * HEAD_DIM, tile] = y.astype(qt_out.dtype)
    yield
    k_t = jnp.concatenate(normed_rotated_heads(A_WIDTH, kg_ref), axis=0)
    k_out[tile, :] = k_t.T.astype(k_out.dtype)
    yield

    vt_out[:, tile] = section_t(2 * A_WIDTH, A_WIDTH).astype(vt_out.dtype)

    u_t = _gelu(section_t(3 * A_WIDTH, B_WIDTH))
    vg_t = _gelu(section_t(3 * A_WIDTH + B_WIDTH, B_WIDTH))
    yield
    n_chunks = tm // GMLP_CHUNK
    j_i = lax.broadcasted_iota(jnp.int32, (GMLP_CHUNK, GMLP_CHUNK), 0)
    i_i = lax.broadcasted_iota(jnp.int32, (GMLP_CHUNK, GMLP_CHUNK), 1)
    keep = j_i <= i_i
    g_rows = []
    for g in range(B_GROUPS):
        rows = slice(g * HEAD_DIM, (g + 1) * HEAD_DIM)
        xg = vg_t[rows]
        xg = xg - jnp.mean(xg, axis=0, keepdims=True)
        var = jnp.mean(xg * xg, axis=0, keepdims=True)
        vn = (xg * lax.rsqrt(var + EPS) * lng_ref[rows] + lnb_ref[rows]).astype(BF16)
        stacked = jnp.concatenate(
            [vn[:, r * GMLP_CHUNK:(r + 1) * GMLP_CHUNK] for r in range(n_chunks)], axis=0)
        w_t = jnp.where(keep, wst_ref[g], 0.0).astype(BF16)
        mixed = _dot(stacked, w_t) + bs_ref[g:g + 1, :]
        mixed = jnp.concatenate(
            [mixed[r * HEAD_DIM:(r + 1) * HEAD_DIM] for r in range(n_chunks)], axis=1)
        g_rows.append(u_t[rows] * mixed)
        if g == B_GROUPS // 2 - 1:
            yield
    g_out[tile, :] = jnp.concatenate(g_rows, axis=0).T.astype(g_out.dtype)


def _ev_params(norm_mix, w_in, q_gain, k_gain, ln_g, ln_b, w_s, b_s):
    n_ev = w_in.shape[0]
    tm = ROW_TILE
    along_lanes = lambda a: jnp.broadcast_to(a[:, :, None], a.shape + (tm,))
    gn = norm_mix[0::2].reshape(n_ev, 1, D_MODEL)
    w_t = jnp.swapaxes(w_in.astype(BF16), 1, 2)
    qg = along_lanes(q_gain)
    kg = along_lanes(k_gain)
    lng = along_lanes(ln_g.reshape(n_ev, B_WIDTH))
    lnb = along_lanes(ln_b.reshape(n_ev, B_WIDTH))
    w_s_t = jnp.swapaxes(w_s, 2, 3)
    return (gn, w_t, qg, kg), (lng, lnb, w_s_t, b_s)


def _ev_proj(x, params, layer, cos_t, sin_t):
    t = x.shape[0]
    tm = ROW_TILE * EV_TILES_PER_STEP
    row_spec = lambda n: pl.BlockSpec((tm, n), lambda i: (i, 0))
    col_spec = pl.BlockSpec((A_WIDTH, tm), lambda i: (0, i))
    rope_spec = pl.BlockSpec((ROPE_DIM // 2, tm), lambda i: (0, i))
    head, tail = params
    return pl.pallas_call(
        _ev_proj_kernel,
        grid=(t // tm,),
        in_specs=([row_spec(D_MODEL)] + [_layer_spec(a, layer) for a in head]
                  + [rope_spec, rope_spec] + [_layer_spec(a, layer) for a in tail]),
        out_specs=[col_spec, row_spec(A_WIDTH), col_spec, row_spec(B_WIDTH)],
        out_shape=[jax.ShapeDtypeStruct((A_WIDTH, t), BF16),
                   jax.ShapeDtypeStruct((t, A_WIDTH), BF16),
                   jax.ShapeDtypeStruct((A_WIDTH, t), BF16),
                   jax.ShapeDtypeStruct((t, B_WIDTH), BF16)],
        compiler_params=_params(1),
        name="ev_proj",
    )(x, *head, cos_t, sin_t, *tail)


def _moba_kernel(qt_ref, k_ref, vt_ref, o_ref, s_ref):
    seq = k_ref.shape[0]
    nb = seq // MOBA_BLOCK
    blk = MOBA_BLOCK
    scale = HEAD_DIM ** -0.5

    k_mean = jnp.mean(k_ref[...].astype(F32).reshape(nb, blk, LANES), axis=1)
    k_mean = jnp.concatenate([k_mean, jnp.zeros((16 - nb, LANES), F32)], axis=0).astype(BF16)

    lo = lax.broadcasted_iota(jnp.int32, (LANES, blk), 0) < HEAD_DIM
    key_i = lax.broadcasted_iota(jnp.int32, (blk, blk), 0)
    qry_i = lax.broadcasted_iota(jnp.int32, (blk, blk), 1)
    causal = key_i <= qry_i

    def scores(i, hd, slot):
        q_i = qt_ref[:, i * blk:(i + 1) * blk]
        head_mask = lo if hd == 0 else jnp.logical_not(lo)
        qh = jnp.where(head_mask, q_i, jnp.zeros_like(q_i)) * scale
        col_max = []
        for j in range(i + 1):
            sj = _dot(k_ref[j * blk:(j + 1) * blk, :], qh)
            if j == i:
                sj = jnp.where(causal, sj, NEG_INF)
            s_ref[slot, j * blk:(j + 1) * blk, :] = sj
            col_max.append(jnp.max(sj, axis=0, keepdims=True))
            yield
        if i > MOBA_TOPK:
            gate = _dot(k_mean, qh)
            g = [gate[j:j + 1, :] for j in range(i)]
            bias = []
            for j in range(i):
                cnt = jnp.zeros_like(g[j])
                for jp in range(i):
                    if jp == j:
                        continue
                    ahead = (g[jp] >= g[j]) if jp < j else (g[jp] > g[j])
                    cnt = cnt + jnp.where(ahead, 1.0, 0.0)
                bias.append(jnp.where(cnt < MOBA_TOPK, 0.0, NEG_INF))
        else:
            bias = [None] * i
        m = col_max[i]
        for j in range(i):
            m = jnp.maximum(m, col_max[j] if bias[j] is None else col_max[j] + bias[j])
        shifts = [-m if bias[j] is None else bias[j] - m for j in range(i)] + [-m]
        return slot, shifts

    def weighted_values(i, hd, slot, shifts):
        ones = jnp.ones((16, blk), BF16)
        acc = jnp.zeros((HEAD_DIM + 16, blk), F32)
        for j in range(i + 1):
            sj = s_ref[slot, j * blk:(j + 1) * blk, :]
            p = jnp.exp(sj + shifts[j]).astype(BF16)
            v_aug = jnp.concatenate(
                [vt_ref[hd * HEAD_DIM:(hd + 1) * HEAD_DIM, j * blk:(j + 1) * blk], ones], axis=0)
            acc = acc + _dot(v_aug, p)
            yield
        return acc[:HEAD_DIM] / acc[HEAD_DIM:HEAD_DIM + 1]

    def interleave(*stages):
        results = [None] * len(stages)
        live = list(enumerate(stages))
        while live:
            for idx, stage in list(live):
                try:
                    next(stage)
                except StopIteration as done:
                    results[idx] = done.value
                    live.remove((idx, stage))
        return results

    groups = [(i, hd) for i in range(nb) for hd in range(LANES // HEAD_DIM)]
    n_slots = s_ref.shape[0]
    pending = interleave(*[scores(*g, n % n_slots)
                           for n, g in enumerate(groups[:MOBA_LOOKAHEAD])])
    head_out = []
    for n, (i, hd) in enumerate(groups):
        stages = [weighted_values(i, hd, *pending.pop(0))]
        if n + MOBA_LOOKAHEAD < len(groups):
            ahead = n + MOBA_LOOKAHEAD
            stages.append(scores(*groups[ahead], ahead % n_slots))
        done = interleave(*stages)
        head_out.append(done[0])
        if len(done) > 1:
            pending.append(done[1])
        if hd == LANES // HEAD_DIM - 1:
            o_t = jnp.concatenate(head_out, axis=0)
            o_ref[i * blk:(i + 1) * blk, :] = o_t.T.astype(o_ref.dtype)
            head_out = []


def _moba(q_t, k, v_t, batch, seq):
    n_pairs = A_WIDTH // LANES
    row_major = pl.BlockSpec((seq, LANES), lambda b, p: (b, p))
    feat_major = pl.BlockSpec((LANES, seq), lambda b, p: (p, b))
    return pl.pallas_call(
        _moba_kernel,
        grid=(batch, n_pairs),
        in_specs=[feat_major, row_major, feat_major],
        out_specs=row_major,
        out_shape=jax.ShapeDtypeStruct(k.shape, BF16),
        scratch_shapes=[pltpu.VMEM((MOBA_LOOKAHEAD + 1, seq, MOBA_BLOCK), F32)],
        compiler_params=_params(2),
        name="moba_attn",
    )(q_t, k, v_t)


def _od_mixer_kernel(x_ref, gn_ref, w_ref, wg1_ref, wg2_ref, bg_ref, og_ref, wo_ref,
                     o_ref, state_ref):
    @pl.when(pl.program_id(1) == 0)
    def _():
        state_ref[...] = jnp.zeros_like(state_ref)

    _alternate([_od_mixer_chain(e, x_ref, gn_ref, w_ref, wg1_ref, wg2_ref, bg_ref, og_ref,
                                wo_ref, o_ref, state_ref) for e in range(x_ref.shape[0])])


def _od_mixer_chain(e, x_ref, gn_ref, w_ref, wg1_ref, wg2_ref, bg_ref, og_ref, wo_ref,
                    o_ref, state_ref):
    tm = x_ref.shape[1]
    c = GLA_CHUNK
    nc = tm // c
    x = x_ref[e]
    h = _rms_rows(x, gn_ref[...]).astype(BF16)
    q = _dot(h, w_ref[:, :GLA_QK_W]) * (GLA_DK ** -0.5)
    k = _dot(h, w_ref[:, GLA_QK_W:2 * GLA_QK_W])
    z = _dot(_dot(h, wg1_ref[...]).astype(BF16), wg2_ref[...]) + bg_ref[...]
    log_a = (jnp.minimum(z, 0.0) - jnp.log(1.0 + jnp.exp(-jnp.abs(z)))) * (1.0 / GLA_TAU)

    in_chunk = lax.broadcasted_iota(jnp.int32, (tm, GLA_QK_W), 0) % c
    b = log_a
    step = 1
    while step < c:
        b = b + jnp.where(in_chunk >= step, pltpu.roll(b, step, 0), 0.0)
        step *= 2
    ab = GLA_ATT_BLOCK
    t_i = lax.broadcasted_iota(jnp.int32, (ab, ab), 0)
    s_i = lax.broadcasted_iota(jnp.int32, (ab, ab), 1)
    causal = jnp.logical_and(t_i // c == s_i // c, s_i <= t_i)
    b3 = b.reshape(nc, c, GLA_QK_W)
    b_last = b3[:, c - 1:c, :]
    q_in = (q * jnp.exp(b)).astype(BF16)
    k_in = (k * jnp.exp(-b)).astype(BF16)
    k_state = (k.reshape(nc, c, GLA_QK_W) * jnp.exp(b_last - b3)).reshape(tm, GLA_QK_W)
    k_state = k_state.astype(BF16)
    decay = jnp.exp(b_last)
    yield

    def intra(hd):
        ks = slice(hd * GLA_DK, (hd + 1) * GLA_DK)
        v0 = 2 * GLA_QK_W + hd * GLA_DV
        r0 = 2 * GLA_QK_W + GLA_V_W + hd * GLA_DV
        v_h = _dot(h, w_ref[:, v0:v0 + GLA_DV]).astype(BF16)
        r_h = _dot(h, w_ref[:, r0:r0 + GLA_DV])
        o_intra = []
        for a0 in range(0, tm, ab):
            rows = slice(a0, a0 + ab)
            att = jnp.where(causal, _dot_nt(q_in[rows, ks], k_in[rows, ks]), 0.0)
            o_intra.append(_dot(att.astype(BF16), v_h[rows]))
        return v_h, r_h, jnp.concatenate(o_intra, axis=0)

    def recurrent(hd, v_h, r_h, o_intra):
        ks = slice(hd * GLA_DK, (hd + 1) * GLA_DK)
        state = state_ref[e, hd]
        parts = []
        for n in range(nc):
            rs = slice(n * c, (n + 1) * c)
            parts.append(o_intra[rs] + _dot_nt(q_in[rs, ks], state.astype(BF16)))
            state = decay[n][:, ks] * state + _dot_tn(v_h[rs], k_state[rs, ks])
        state_ref[e, hd] = state
        o = jnp.concatenate(parts, axis=0)
        o = _rms_rows(o, og_ref[:, hd * GLA_DV:(hd + 1) * GLA_DV])
        return (o * _silu(r_h)).astype(BF16)

    heads = []
    for hd in range(GLA_HEADS):
        pre = intra(hd)
        yield
        heads.append(recurrent(hd, *pre))
        yield
    o_ref[e] = x + _dot(jnp.concatenate(heads, axis=-1), wo_ref[...])


def _od_params(norm_mix, w_in, w_g1, w_g2, b_g, o_gain, w_out):
    n_od = w_in.shape[0]
    wg1 = jnp.pad(w_g1, ((0, 0), (0, 0), (0, LANES - GLA_LOWRANK))).astype(BF16)
    wg2 = jnp.pad(w_g2, ((0, 0), (0, LANES - GLA_LOWRANK), (0, 0))).astype(BF16)
    return (norm_mix[1::2].reshape(n_od, 1, D_MODEL), w_in.astype(BF16), wg1, wg2,
            b_g.reshape(n_od, 1, GLA_QK_W), o_gain.reshape(n_od, 1, GLA_V_W),
            w_out.astype(BF16))


def _od_mixer(x, params, layer, batch, seq):
    tm = ROW_TILE
    pair = GLA_BATCH_PAIR
    assert batch % pair == 0
    x3 = x.reshape(batch, seq, D_MODEL)
    tile_spec = pl.BlockSpec((pair, tm, D_MODEL), lambda b, i: (b, i, 0))
    out = pl.pallas_call(
        _od_mixer_kernel,
        grid=(batch // pair, seq // tm),
        in_specs=[tile_spec] + [_layer_spec(a, layer) for a in params],
        out_specs=tile_spec,
        out_shape=jax.ShapeDtypeStruct(x3.shape, F32),
        scratch_shapes=[pltpu.VMEM((pair, GLA_HEADS, GLA_DV, GLA_DK), F32)],
        compiler_params=_params(2),
        name="gla_mixer",
    )(x3, *params)
    return out.reshape(x.shape)


def _mem_kv_kernel(m_ref, gn_ref, w_ref, kg_ref, k_out, v_out):
    h = _rms_rows(m_ref[...], gn_ref[...]).astype(BF16)
    k = _dot(h, w_ref[:, :D_MODEL])
    for hh in range(X_HEADS):
        sl = slice(hh * X_HEAD_DIM, (hh + 1) * X_HEAD_DIM)
        k_out[:, sl] = _rms_rows(k[:, sl], kg_ref[...]).astype(k_out.dtype)
    v_out[...] = _dot(h, w_ref[:, D_MODEL:]).astype(v_out.dtype)


def _mem_kv(mem, gn, w_kv, k_gain):
    t = mem.shape[0]
    depth = w_kv.shape[0]
    tm = ROW_TILE
    per_layer = lambda a: pl.BlockSpec((None,) + a.shape[1:], lambda l, i: (l, 0, 0))
    out_spec = pl.BlockSpec((None, tm, D_MODEL), lambda l, i: (l, i, 0))
    return pl.pallas_call(
        _mem_kv_kernel,
        grid=(depth, t // tm),
        in_specs=[pl.BlockSpec((tm, D_MODEL), lambda l, i: (i, 0)),
                  per_layer(gn), per_layer(w_kv), per_layer(k_gain)],
        out_specs=[out_spec, out_spec],
        out_shape=[jax.ShapeDtypeStruct((depth, t, D_MODEL), BF16)] * 2,
        compiler_params=_params(2),
        name="mem_kv",
    )(mem, gn, w_kv, k_gain)


def _xattn_ffn_body(x, gq_ref, wq_ref, qg_ref, k_ref, v_ref, wo_ref,
                    gf_ref, wgu_ref, wd_ref, o_ref):
    h = _rms_rows(x, gq_ref[...]).astype(BF16)
    q = _dot(h, wq_ref[...])
    scale = X_HEAD_DIM ** -0.5
    heads = []
    for hh in range(X_HEADS):
        sl = slice(hh * X_HEAD_DIM, (hh + 1) * X_HEAD_DIM)
        qh = _rms_rows(q[:, sl], qg_ref[...]).astype(BF16)
        s = _dot_nt(qh, k_ref[:, sl]) * scale
        e = jnp.exp(s - jnp.max(s, axis=-1, keepdims=True))
        p = (e / jnp.sum(e, axis=-1, keepdims=True)).astype(BF16)
        heads.append(_dot(p, v_ref[:, sl]).astype(BF16))
    x = x + _dot(jnp.concatenate(heads, axis=-1), wo_ref[...])

    h = _rms_rows(x, gf_ref[...]).astype(BF16)
    acc = x
    for lo, hi in FFN_SPLITS:
        g = _dot(h, wgu_ref[:, lo:hi])
        u = _dot(h, wgu_ref[:, D_FF + lo:D_FF + hi])
        acc = acc + _dot((_silu(g) * u).astype(BF16), wd_ref[lo:hi, :])
    o_ref[...] = acc


POST_WEIGHTS = ("w_q", "w_xo", "w_gu", "w_down")
N_POST_OPERANDS = 9


def _post_kernel(*refs, has_proj, n_cast):
    refs = list(refs)
    x = refs.pop(0)[...]
    if has_proj:
        a_ref, g_ref, wm_ref = refs[:3]
        del refs[:3]
        n_a = a_ref.shape[1]
        x = x + _dot(a_ref[...], wm_ref[:n_a, :]) + _dot(g_ref[...], wm_ref[n_a:, :])
    operands = refs[:N_POST_OPERANDS]
    cast_src = refs[N_POST_OPERANDS:N_POST_OPERANDS + n_cast]
    o_ref = refs[N_POST_OPERANDS + n_cast]
    cast_dst = refs[N_POST_OPERANDS + n_cast + 1:]
    for src, dst in zip(cast_src, cast_dst):
        dst[...] = src[...].astype(dst.dtype)
    _xattn_ffn_body(x, *operands, o_ref)


def _resident_spec(arr):
    nd = arr.ndim
    return pl.BlockSpec(arr.shape, lambda *_: (0,) * nd, pipeline_mode=pl.Buffered(1))


def _xattn_ffn(x, mixer, p, w, nxt, layer, k_mem, v_mem, batch, seq):
    tm = ROW_TILE
    steps = seq // tm
    n_steps = batch * steps
    flat = lambda b, i: b * steps + i
    row_spec = lambda n: pl.BlockSpec((tm, n), lambda b, i: (flat(b, i), 0))
    mem_spec = pl.BlockSpec((None, N_MEM, D_MODEL), lambda b, i: (layer, b, 0))
    ls = lambda name: _layer_spec(p[name], layer)
    specs = [row_spec(D_MODEL)]
    args = [x]
    if mixer is not None:
        a, g, w_m, m_layer = mixer
        specs += [row_spec(a.shape[1]), row_spec(g.shape[1]), _layer_spec(w_m, m_layer)]
        args += [a, g, w_m]
    specs += [ls("gq"), _resident_spec(w["w_q"]), ls("q_gain"), mem_spec, mem_spec,
              _resident_spec(w["w_xo"]), ls("gf"), _resident_spec(w["w_gu"]),
              _resident_spec(w["w_down"])]
    args += [p["gq"], w["w_q"], p["q_gain"], k_mem, v_mem, w["w_xo"], p["gf"], w["w_gu"],
             w["w_down"]]
    out_specs = [row_spec(D_MODEL)]
    out_shape = [jax.ShapeDtypeStruct(x.shape, F32)]
    n_cast = 0
    if nxt is not None:
        for name in POST_WEIGHTS:
            _, rows, cols = nxt[name].shape
            visits = 1 if rows % (16 * n_steps) == 0 else 2
            band = rows * visits // n_steps
            assert band % 16 == 0 and band * n_steps == rows * visits
            specs.append(pl.BlockSpec((None, band, cols),
                                      lambda b, i, v=visits: (layer + 1, flat(b, i) // v, 0)))
            args.append(nxt[name])
            out_specs.append(pl.BlockSpec((band, cols),
                                          lambda b, i, v=visits: (flat(b, i) // v, 0)))
            out_shape.append(jax.ShapeDtypeStruct((rows, cols), BF16))
        n_cast = len(POST_WEIGHTS)
    outs = pl.pallas_call(
        partial(_post_kernel, has_proj=mixer is not None, n_cast=n_cast),
        grid=(batch, steps),
        in_specs=specs,
        out_specs=out_specs,
        out_shape=out_shape,
        compiler_params=_params(2),
        name="xattn_ffn" if mixer is None else "proj_xattn_ffn",
    )(*args)
    nxt_w = dict(zip(POST_WEIGHTS, outs[1:])) if nxt is not None else None
    return outs[0], nxt_w


def kernel(x, mem, positions, norm_mix, norm_mem_q, norm_mem_kv, norm_ffn, ev_w_in, ev_q_gain, ev_k_gain, ev_w_s, ev_b_s, ev_ln_g, ev_ln_b, ev_w_out, od_w_in, od_w_g1, od_w_g2, od_b_g, od_o_gain, od_w_out, xa_w_q, xa_w_kv, xa_q_gain, xa_k_gain, xa_w_out, ffn_w_gu, ffn_w_down):
    batch, seq, d_model = x.shape
    assert d_model == D_MODEL and seq % ROW_TILE == 0 and seq % MOBA_BLOCK == 0
    assert mem.shape == (batch, N_MEM, D_MODEL)
    xf = x.reshape(batch * seq, D_MODEL)
    memf = mem.reshape(batch * N_MEM, D_MODEL)
    cos_t, sin_t = _rope_tables(positions)
    vec = lambda a: a.reshape(a.shape[0], 1, a.shape[1])
    ev = _ev_params(norm_mix, ev_w_in, ev_q_gain, ev_k_gain, ev_ln_g, ev_ln_b, ev_w_s, ev_b_s)
    ev_w_out_bf = ev_w_out.astype(BF16)
    od = _od_params(norm_mix, od_w_in, od_w_g1, od_w_g2, od_b_g, od_o_gain, od_w_out)
    post = dict(gq=vec(norm_mem_q), q_gain=vec(xa_q_gain), gf=vec(norm_ffn))
    post_f32 = dict(w_q=xa_w_q, w_xo=xa_w_out, w_gu=ffn_w_gu, w_down=ffn_w_down)
    post_w = {name: a[0].astype(BF16) for name, a in post_f32.items()}
    k_mem, v_mem = _mem_kv(memf, vec(norm_mem_kv), xa_w_kv.astype(BF16), vec(xa_k_gain))
    for l in range(DEPTH):
        i = l // 2
        if l % 2 == 0:
            q_t, k, v_t, g = _ev_proj(xf, ev, i, cos_t, sin_t)
            mixer = (_moba(q_t, k, v_t, batch, seq), g, ev_w_out_bf, i)
        else:
            xf = _od_mixer(xf, od, i, batch, seq)
            mixer = None
        xf, post_w = _xattn_ffn(xf, mixer, post, post_w, post_f32 if l + 1 < DEPTH else None,
                                l, k_mem, v_mem, batch, seq)
    return xf.reshape(batch, seq, D_MODEL)
```

```python
import math
from functools import partial

import jax
import jax.numpy as jnp
from jax import lax
from jax.experimental import pallas as pl
from jax.experimental.pallas import tpu as pltpu

F32 = jnp.float32
BF16 = jnp.bfloat16

D_MODEL = 1024
DEPTH = 4
N_MEM = 256
EPS = 1e-6
NEG_INF = -1e30

HEAD_DIM = 64
ROPE_DIM = HEAD_DIM // 4
ROPE_THETA = 500000.0

A_HEADS = 8
A_WIDTH = A_HEADS * HEAD_DIM
MOBA_BLOCK = 256
MOBA_TOPK = 3
MOBA_LOOKAHEAD = 3

B_GROUPS = 8
B_WIDTH = B_GROUPS * HEAD_DIM
GMLP_CHUNK = 128

GLA_HEADS = 4
GLA_DK = (D_MODEL // 2) // GLA_HEADS
GLA_DV = D_MODEL // GLA_HEADS
GLA_QK_W = GLA_HEADS * GLA_DK
GLA_V_W = GLA_HEADS * GLA_DV
GLA_LOWRANK = 16
GLA_TAU = 16.0
GLA_CHUNK = 64
GLA_BATCH_PAIR = 2
GLA_ATT_BLOCK = 256

X_HEADS = 4
X_HEAD_DIM = D_MODEL // X_HEADS

D_FF = -(-8 * D_MODEL // (3 * 256)) * 256

LANES = 128
VMEM_LIMIT_BYTES = 56 * 1024 * 1024

ROW_TILE = 512
EV_TILES_PER_STEP = 4
FFN_SPLITS = tuple((lo, lo + 256) for lo in range(0, D_FF, 256))


def _params(n_axes):
    return pltpu.CompilerParams(
        dimension_semantics=("arbitrary",) * n_axes,
        vmem_limit_bytes=VMEM_LIMIT_BYTES)


def _layer_spec(stacked, layer):
    nd = stacked.ndim - 1
    return pl.BlockSpec((None,) + stacked.shape[1:], lambda *_: (layer,) + (0,) * nd,
                        pipeline_mode=pl.Buffered(1))


def _dot(a, b):
    return jnp.dot(a, b, preferred_element_type=F32)


def _dot_nt(a, b):
    return lax.dot_general(a, b, (((1,), (1,)), ((), ())),
                           preferred_element_type=F32)


def _dot_tn(a, b):
    return lax.dot_general(a, b, (((0,), (0,)), ((), ())),
                           preferred_element_type=F32)


def _rms_rows(x, gain):
    ms = jnp.mean(x * x, axis=-1, keepdims=True)
    return x * lax.rsqrt(ms + EPS) * gain


def _gelu(x):
    return 0.5 * x * (1.0 + lax.erf(x * math.sqrt(0.5)))


def _silu(x):
    return x / (1.0 + jnp.exp(-x))


def _rope_table_kernel(pos_ref, invf_ref, cos_ref, sin_ref):
    ang = pos_ref[...] * invf_ref[...]
    cos_ref[...] = jnp.cos(ang)
    sin_ref[...] = jnp.sin(ang)


def _rope_tables(positions):
    t = positions.size
    half = ROPE_DIM // 2
    inv_freq = ROPE_THETA ** (-jnp.arange(half, dtype=F32) * 2.0 / ROPE_DIM)
    invf = jnp.broadcast_to(inv_freq[:, None], (half, t))
    pos = positions.astype(F32).reshape(1, t)
    whole = lambda rows: pl.BlockSpec((rows, t), lambda: (0, 0))
    return pl.pallas_call(
        _rope_table_kernel,
        in_specs=[whole(1), whole(half)],
        out_specs=[whole(half), whole(half)],
        out_shape=[jax.ShapeDtypeStruct((half, t), F32)] * 2,
        name="rope_tables",
    )(pos, invf)


def _alternate(chains):
    next(chains[0])
    live = list(chains)
    while live:
        for chain in list(live[::-1]):
            try:
                next(chain)
            except StopIteration:
                live.remove(chain)


def _ev_proj_kernel(x_ref, *refs):
    _alternate([_ev_proj_chain(e, x_ref, *refs) for e in range(x_ref.shape[0] // ROW_TILE)])


def _ev_proj_chain(e, x_ref, gn_ref, wt_ref, qg_ref, kg_ref, cos_ref, sin_ref,
                   lng_ref, lnb_ref, wst_ref, bs_ref,
                   qt_out, k_out, vt_out, g_out):
    tm = ROW_TILE
    tile = slice(e * tm, (e + 1) * tm)
    h = _rms_rows(x_ref[tile, :], gn_ref[...]).astype(BF16)
    cos_t = cos_ref[:, tile]
    sin_t = sin_ref[:, tile]
    half = ROPE_DIM // 2

    def section_t(row0, width):
        return _dot_nt(wt_ref[row0:row0 + width, :], h)

    def normed_rotated_heads(row0, gain_ref):
        sec = section_t(row0, A_WIDTH)
        heads = []
        for hd in range(A_HEADS):
            xh = sec[hd * HEAD_DIM:(hd + 1) * HEAD_DIM]
            ms = jnp.mean(xh * xh, axis=0, keepdims=True)
            y = xh * lax.rsqrt(ms + EPS) * gain_ref[...]
            x1, x2 = y[:half], y[half:ROPE_DIM]
            heads.append(jnp.concatenate(
                [x1 * cos_t - x2 * sin_t, x2 * cos_t + x1 * sin_t, y[ROPE_DIM:]], axis=0))
        return heads

    for hd, y in enumerate(normed_rotated_heads(0, qg_ref)):
        qt_out[hd * HEAD_DIM:(hd + 1) * HEAD_DIM, tile] = y.astype(qt_out.dtype)
    yield
    k_t = jnp.concatenate(normed_rotated_heads(A_WIDTH, kg_ref), axis=0)
    k_out[tile, :] = k_t.T.astype(k_out.dtype)
    yield

    vt_out[:, tile] = section_t(2 * A_WIDTH, A_WIDTH).astype(vt_out.dtype)

    u_t = _gelu(section_t(3 * A_WIDTH, B_WIDTH))
    vg_t = _gelu(section_t(3 * A_WIDTH + B_WIDTH, B_WIDTH))
    yield
    n_chunks = tm // GMLP_CHUNK
    j_i = lax.broadcasted_iota(jnp.int32, (GMLP_CHUNK, GMLP_CHUNK), 0)
    i_i = lax.broadcasted_iota(jnp.int32, (GMLP_CHUNK, GMLP_CHUNK), 1)
    keep = j_i <= i_i
    g_rows = []
    for g in range(B_GROUPS):
        rows = slice(g * HEAD_DIM, (g + 1) * HEAD_DIM)
        xg = vg_t[rows]
        xg = xg - jnp.mean(xg, axis=0, keepdims=True)
        var = jnp.mean(xg * xg, axis=0, keepdims=True)
        vn = (xg * lax.rsqrt(var + EPS) * lng_ref[rows] + lnb_ref[rows]).astype(BF16)
        stacked = jnp.concatenate(
            [vn[:, r * GMLP_CHUNK:(r + 1) * GMLP_CHUNK] for r in range(n_chunks)], axis=0)
        w_t = jnp.where(keep, wst_ref[g], 0.0).astype(BF16)
        mixed = _dot(stacked, w_t) + bs_ref[g:g + 1, :]
        mixed = jnp.concatenate(
            [mixed[r * HEAD_DIM:(r + 1) * HEAD_DIM] for r in range(n_chunks)], axis=1)
        g_rows.append(u_t[rows] * mixed)
        if g == B_GROUPS // 2 - 1:
            yield
    g_out[tile, :] = jnp.concatenate(g_rows, axis=0).T.astype(g_out.dtype)


def _ev_params(norm_mix, w_in, q_gain, k_gain, ln_g, ln_b, w_s, b_s):
    n_ev = w_in.shape[0]
    tm = ROW_TILE
    along_lanes = lambda a: jnp.broadcast_to(a[:, :, None], a.shape + (tm,))
    gn = norm_mix[0::2].reshape(n_ev, 1, D_MODEL)
    w_t = jnp.swapaxes(w_in.astype(BF16), 1, 2)
    qg = along_lanes(q_gain)
    kg = along_lanes(k_gain)
    lng = along_lanes(ln_g.reshape(n_ev, B_WIDTH))
    lnb = along_lanes(ln_b.reshape(n_ev, B_WIDTH))
    w_s_t = jnp.swapaxes(w_s, 2, 3)
    return (gn, w_t, qg, kg), (lng, lnb, w_s_t, b_s)


def _ev_proj(x, params, layer, cos_t, sin_t):
    t = x.shape[0]
    tm = ROW_TILE * EV_TILES_PER_STEP
    row_spec = lambda n: pl.BlockSpec((tm, n), lambda i: (i, 0))
    col_spec = pl.BlockSpec((A_WIDTH, tm), lambda i: (0, i))
    rope_spec = pl.BlockSpec((ROPE_DIM // 2, tm), lambda i: (0, i))
    head, tail = params
    return pl.pallas_call(
        _ev_proj_kernel,
        grid=(t // tm,),
        in_specs=([row_spec(D_MODEL)] + [_layer_spec(a, layer) for a in head]
                  + [rope_spec, rope_spec] + [_layer_spec(a, layer) for a in tail]),
        out_specs=[col_spec, row_spec(A_WIDTH), col_spec, row_spec(B_WIDTH)],
        out_shape=[jax.ShapeDtypeStruct((A_WIDTH, t), BF16),
                   jax.ShapeDtypeStruct((t, A_WIDTH), BF16),
                   jax.ShapeDtypeStruct((A_WIDTH, t), BF16),
                   jax.ShapeDtypeStruct((t, B_WIDTH), BF16)],
        compiler_params=_params(1),
        name="ev_proj",
    )(x, *head, cos_t, sin_t, *tail)


def _moba_kernel(qt_ref, k_ref, vt_ref, o_ref, s_ref):
    seq = k_ref.shape[0]
    nb = seq // MOBA_BLOCK
    blk = MOBA_BLOCK
    scale = HEAD_DIM ** -0.5

    k_mean = jnp.mean(k_ref[...].astype(F32).reshape(nb, blk, LANES), axis=1)
    k_mean = jnp.concatenate([k_mean, jnp.zeros((16 - nb, LANES), F32)], axis=0).astype(BF16)

    lo = lax.broadcasted_iota(jnp.int32, (LANES, blk), 0) < HEAD_DIM
    key_i = lax.broadcasted_iota(jnp.int32, (blk, blk), 0)
    qry_i = lax.broadcasted_iota(jnp.int32, (blk, blk), 1)
    causal = key_i <= qry_i

    def scores(i, hd, slot):
        q_i = qt_ref[:, i * blk:(i + 1) * blk]
        head_mask = lo if hd == 0 else jnp.logical_not(lo)
        qh = jnp.where(head_mask, q_i, jnp.zeros_like(q_i)) * scale
        col_max = []
        for j in range(i + 1):
            sj = _dot(k_ref[j * blk:(j + 1) * blk, :], qh)
            if j == i:
                sj = jnp.where(causal, sj, NEG_INF)
            s_ref[slot, j * blk:(j + 1) * blk, :] = sj
            col_max.append(jnp.max(sj, axis=0, keepdims=True))
            yield
        if i > MOBA_TOPK:
            gate = _dot(k_mean, qh)
            g = [gate[j:j + 1, :] for j in range(i)]
            bias = []
            for j in range(i):
                cnt = jnp.zeros_like(g[j])
                for jp in range(i):
                    if jp == j:
                        continue
                    ahead = (g[jp] >= g[j]) if jp < j else (g[jp] > g[j])
                    cnt = cnt + jnp.where(ahead, 1.0, 0.0)
                bias.append(jnp.where(cnt < MOBA_TOPK, 0.0, NEG_INF))
        else:
            bias = [None] * i
        m = col_max[i]
        for j in range(i):
            m = jnp.maximum(m, col_max[j] if bias[j] is None else col_max[j] + bias[j])
        shifts = [-m if bias[j] is None else bias[j] - m for j in range(i)] + [-m]
        return slot, shifts

    def weighted_values(i, hd, slot, shifts):
        ones = jnp.ones((16, blk), BF16)
        acc = jnp.zeros((HEAD_DIM + 16, blk), F32)
        for j in range(i + 1):
            sj = s_ref[slot, j * blk:(j + 1) * blk, :]
            p = jnp.exp(sj + shifts[j]).astype(BF16)
            v_aug = jnp.concatenate(
                [vt_ref[hd * HEAD_DIM:(hd + 1) * HEAD_DIM, j * blk:(j + 1) * blk], ones], axis=0)
            acc = acc + _dot(v_aug, p)
            yield
        return acc[:HEAD_DIM] / acc[HEAD_DIM:HEAD_DIM + 1]

    def interleave(*stages):
        results = [None] * len(stages)
        live = list(enumerate(stages))
        while live:
            for idx, stage in list(live):
                try:
                    next(stage)
                except StopIteration as done:
                    results[idx] = done.value
                    live.remove((idx, stage))
        return results

    groups = [(i, hd) for i in range(nb) for hd in range(LANES // HEAD_DIM)]
    n_slots = s_ref.shape[0]
    pending = interleave(*[scores(*g, n % n_slots)
                           for n, g in enumerate(groups[:MOBA_LOOKAHEAD])])
    head_out = []
    for n, (i, hd) in enumerate(groups):
        stages = [weighted_values(i, hd, *pending.pop(0))]
        if n + MOBA_LOOKAHEAD < len(groups):
            ahead = n + MOBA_LOOKAHEAD
            stages.append(scores(*groups[ahead], ahead % n_slots))
        done = interleave(*stages)
        head_out.append(done[0])
        if len(done) > 1:
            pending.append(done[1])
        if hd == LANES // HEAD_DIM - 1:
            o_t = jnp.concatenate(head_out, axis=0)
            o_ref[i * blk:(i + 1) * blk, :] = o_t.T.astype(o_ref.dtype)
            head_out = []


def _cast_plan(sources, n_steps, flat_step):
    in_specs, out_specs, out_shapes = [], [], []
    for arr, layer in sources:
        rows, cols = arr.shape[-2:]
        visits = 1 if rows % (16 * n_steps) == 0 else 2
        band = rows * visits // n_steps
        assert band % 16 == 0 and band * n_steps == rows * visits
        if layer is None:
            in_specs.append(pl.BlockSpec(
                (band, cols), lambda *g, v=visits: (flat_step(*g) // v, 0)))
        else:
            in_specs.append(pl.BlockSpec(
                (None, band, cols), lambda *g, v=visits, l=layer: (l, flat_step(*g) // v, 0)))
        out_specs.append(pl.BlockSpec((band, cols), lambda *g, v=visits: (flat_step(*g) // v, 0)))
        out_shapes.append(jax.ShapeDtypeStruct((rows, cols), BF16))
    return in_specs, out_specs, out_shapes


def _moba_cast_kernel(qt_ref, k_ref, vt_ref, *refs, n_cast):
    cast_src, o_ref, cast_dst, s_ref = (refs[:n_cast], refs[n_cast],
                                        refs[n_cast + 1:2 * n_cast + 1], refs[-1])
    for src, dst in zip(cast_src, cast_dst):
        dst[...] = src[...].astype(dst.dtype)
    _moba_kernel(qt_ref, k_ref, vt_ref, o_ref, s_ref)


def _moba(q_t, k, v_t, batch, seq, cast_sources=()):
    n_pairs = A_WIDTH // LANES
    row_major = pl.BlockSpec((seq, LANES), lambda b, p: (b, p))
    feat_major = pl.BlockSpec((LANES, seq), lambda b, p: (p, b))
    c_in, c_out, c_shapes = _cast_plan(cast_sources, batch * n_pairs,
                                       lambda b, p: b * n_pairs + p)
    outs = pl.pallas_call(
        partial(_moba_cast_kernel, n_cast=len(cast_sources)),
        grid=(batch, n_pairs),
        in_specs=[feat_major, row_major, feat_major] + c_in,
        out_specs=[row_major] + c_out,
        out_shape=[jax.ShapeDtypeStruct(k.shape, BF16)] + c_shapes,
        scratch_shapes=[pltpu.VMEM((MOBA_LOOKAHEAD + 1, seq, MOBA_BLOCK), F32)],
        compiler_params=_params(2),
        name="moba_attn",
    )(q_t, k, v_t, *[arr for arr, _ in cast_sources])
    return outs[0], outs[1:]


def _od_mixer_kernel(x_ref, gn_ref, w_ref, wg1_ref, wg2_ref, bg_ref, og_ref, wo_ref,
                     o_ref, state_ref):
    @pl.when(pl.program_id(1) == 0)
    def _():
        state_ref[...] = jnp.zeros_like(state_ref)

    _alternate([_od_mixer_chain(e, x_ref, gn_ref, w_ref, wg1_ref, wg2_ref, bg_ref, og_ref,
                                wo_ref, o_ref, state_ref) for e in range(x_ref.shape[0])])


def _od_mixer_chain(e, x_ref, gn_ref, w_ref, wg1_ref, wg2_ref, bg_ref, og_ref, wo_ref,
                    o_ref, state_ref):
    tm = x_ref.shape[1]
    c = GLA_CHUNK
    nc = tm // c
    x = x_ref[e]
    h = _rms_rows(x, gn_ref[...]).astype(BF16)
    q = _dot(h, w_ref[:, :GLA_QK_W]) * (GLA_DK ** -0.5)
    k = _dot(h, w_ref[:, GLA_QK_W:2 * GLA_QK_W])
    z = _dot(_dot(h, wg1_ref[...]).astype(BF16), wg2_ref[...]) + bg_ref[...]
    log_a = (jnp.minimum(z, 0.0) - jnp.log(1.0 + jnp.exp(-jnp.abs(z)))) * (1.0 / GLA_TAU)

    in_chunk = lax.broadcasted_iota(jnp.int32, (tm, GLA_QK_W), 0) % c
    b = log_a
    step = 1
    while step < c:
        b = b + jnp.where(in_chunk >= step, pltpu.roll(b, step, 0), 0.0)
        step *= 2
    ab = GLA_ATT_BLOCK
    t_i = lax.broadcasted_iota(jnp.int32, (ab, ab), 0)
    s_i = lax.broadcasted_iota(jnp.int32, (ab, ab), 1)
    causal = jnp.logical_and(t_i // c == s_i // c, s_i <= t_i)
    b3 = b.reshape(nc, c, GLA_QK_W)
    b_last = b3[:, c - 1:c, :]
    q_in = (q * jnp.exp(b)).astype(BF16)
    k_in = (k * jnp.exp(-b)).astype(BF16)
    k_state = (k.reshape(nc, c, GLA_QK_W) * jnp.exp(b_last - b3)).reshape(tm, GLA_QK_W)
    k_state = k_state.astype(BF16)
    decay = jnp.exp(b_last)
    yield

    def intra(hd):
        ks = slice(hd * GLA_DK, (hd + 1) * GLA_DK)
        v0 = 2 * GLA_QK_W + hd * GLA_DV
        r0 = 2 * GLA_QK_W + GLA_V_W + hd * GLA_DV
        v_h = _dot(h, w_ref[:, v0:v0 + GLA_DV]).astype(BF16)
        r_h = _dot(h, w_ref[:, r0:r0 + GLA_DV])
        o_intra = []
        for a0 in range(0, tm, ab):
            rows = slice(a0, a0 + ab)
            att = jnp.where(causal, _dot_nt(q_in[rows, ks], k_in[rows, ks]), 0.0)
            o_intra.append(_dot(att.astype(BF16), v_h[rows]))
        return v_h, r_h, jnp.concatenate(o_intra, axis=0)

    def recurrent(hd, v_h, r_h, o_intra):
        ks = slice(hd * GLA_DK, (hd + 1) * GLA_DK)
        state = state_ref[e, hd]
        parts = []
        for n in range(nc):
            rs = slice(n * c, (n + 1) * c)
            parts.append(o_intra[rs] + _dot_nt(q_in[rs, ks], state.astype(BF16)))
            state = decay[n][:, ks] * state + _dot_tn(v_h[rs], k_state[rs, ks])
        state_ref[e, hd] = state
        o = jnp.concatenate(parts, axis=0)
        o = _rms_rows(o, og_ref[:, hd * GLA_DV:(hd + 1) * GLA_DV])
        return (o * _silu(r_h)).astype(BF16)

    heads = []
    for hd in range(GLA_HEADS):
        pre = intra(hd)
        yield
        heads.append(recurrent(hd, *pre))
        yield
    o_ref[e] = x + _dot(jnp.concatenate(heads, axis=-1), wo_ref[...])


def _od_params(norm_mix, w_in, w_g1, w_g2, b_g, o_gain, w_out):
    n_od = w_in.shape[0]
    wg1 = jnp.pad(w_g1, ((0, 0), (0, 0), (0, LANES - GLA_LOWRANK))).astype(BF16)
    wg2 = jnp.pad(w_g2, ((0, 0), (0, LANES - GLA_LOWRANK), (0, 0))).astype(BF16)
    return (norm_mix[1::2].reshape(n_od, 1, D_MODEL), w_in.astype(BF16), wg1, wg2,
            b_g.reshape(n_od, 1, GLA_QK_W), o_gain.reshape(n_od, 1, GLA_V_W),
            w_out.astype(BF16))


def _od_mixer(x, params, layer, batch, seq):
    tm = ROW_TILE
    pair = GLA_BATCH_PAIR
    assert batch % pair == 0
    x3 = x.reshape(batch, seq, D_MODEL)
    tile_spec = pl.BlockSpec((pair, tm, D_MODEL), lambda b, i: (b, i, 0))
    out = pl.pallas_call(
        _od_mixer_kernel,
        grid=(batch // pair, seq // tm),
        in_specs=[tile_spec] + [_layer_spec(a, layer) for a in params],
        out_specs=tile_spec,
        out_shape=jax.ShapeDtypeStruct(x3.shape, F32),
        scratch_shapes=[pltpu.VMEM((pair, GLA_HEADS, GLA_DV, GLA_DK), F32)],
        compiler_params=_params(2),
        name="gla_mixer",
    )(x3, *params)
    return out.reshape(x.shape)


def _mem_kv_kernel(m_ref, gn_ref, w_ref, kg_ref, k_out, v_out):
    h = _rms_rows(m_ref[...], gn_ref[...]).astype(BF16)
    k = _dot(h, w_ref[:, :D_MODEL])
    for hh in range(X_HEADS):
        sl = slice(hh * X_HEAD_DIM, (hh + 1) * X_HEAD_DIM)
        k_out[:, sl] = _rms_rows(k[:, sl], kg_ref[...]).astype(k_out.dtype)
    v_out[...] = _dot(h, w_ref[:, D_MODEL:]).astype(v_out.dtype)


def _mem_kv(mem, gn, w_kv, k_gain):
    t = mem.shape[0]
    depth = w_kv.shape[0]
    tm = ROW_TILE
    per_layer = lambda a: pl.BlockSpec((None,) + a.shape[1:], lambda l, i: (l, 0, 0))
    out_spec = pl.BlockSpec((None, tm, D_MODEL), lambda l, i: (l, i, 0))
    return pl.pallas_call(
        _mem_kv_kernel,
        grid=(depth, t // tm),
        in_specs=[pl.BlockSpec((tm, D_MODEL), lambda l, i: (i, 0)),
                  per_layer(gn), per_layer(w_kv), per_layer(k_gain)],
        out_specs=[out_spec, out_spec],
        out_shape=[jax.ShapeDtypeStruct((depth, t, D_MODEL), BF16)] * 2,
        compiler_params=_params(2),
        name="mem_kv",
    )(mem, gn, w_kv, k_gain)


def _xattn_ffn_body(x, gq_ref, wq_ref, qg_ref, k_ref, v_ref, wo_ref,
                    gf_ref, wgu_ref, wd_ref, o_ref):
    h = _rms_rows(x, gq_ref[...]).astype(BF16)
    q = _dot(h, wq_ref[...])
    scale = X_HEAD_DIM ** -0.5
    heads = []
    for hh in range(X_HEADS):
        sl = slice(hh * X_HEAD_DIM, (hh + 1) * X_HEAD_DIM)
        qh = _rms_rows(q[:, sl], qg_ref[...]).astype(BF16)
        s = _dot_nt(qh, k_ref[:, sl]) * scale
        e = jnp.exp(s - jnp.max(s, axis=-1, keepdims=True))
        p = (e / jnp.sum(e, axis=-1, keepdims=True)).astype(BF16)
        heads.append(_dot(p, v_ref[:, sl]).astype(BF16))
    x = x + _dot(jnp.concatenate(heads, axis=-1), wo_ref[...])

    h = _rms_rows(x, gf_ref[...]).astype(BF16)
    acc = x
    for lo, hi in FFN_SPLITS:
        g = _dot(h, wgu_ref[:, lo:hi])
        u = _dot(h, wgu_ref[:, D_FF + lo:D_FF + hi])
        acc = acc + _dot((_silu(g) * u).astype(BF16), wd_ref[lo:hi, :])
    o_ref[...] = acc


POST_WEIGHTS = ("w_q", "w_xo", "w_gu", "w_down")
N_POST_OPERANDS = 9


def _post_kernel(*refs, has_proj, n_cast):
    refs = list(refs)
    x = refs.pop(0)[...]
    if has_proj:
        a_ref, g_ref, wm_ref = refs[:3]
        del refs[:3]
        n_a = a_ref.shape[1]
        x = x + _dot(a_ref[...], wm_ref[:n_a, :]) + _dot(g_ref[...], wm_ref[n_a:, :])
    operands = refs[:N_POST_OPERANDS]
    cast_src = refs[N_POST_OPERANDS:N_POST_OPERANDS + n_cast]
    o_ref = refs[N_POST_OPERANDS + n_cast]
    cast_dst = refs[N_POST_OPERANDS + n_cast + 1:]
    for src, dst in zip(cast_src, cast_dst):
        dst[...] = src[...].astype(dst.dtype)
    _xattn_ffn_body(x, *operands, o_ref)


def _resident_spec(arr):
    nd = arr.ndim
    return pl.BlockSpec(arr.shape, lambda *_: (0,) * nd, pipeline_mode=pl.Buffered(1))


def _xattn_ffn(x, mixer, p, w, nxt, layer, k_mem, v_mem, batch, seq):
    tm = ROW_TILE
    steps = seq // tm
    n_steps = batch * steps
    flat = lambda b, i: b * steps + i
    row_spec = lambda n: pl.BlockSpec((tm, n), lambda b, i: (flat(b, i), 0))
    mem_spec = pl.BlockSpec((None, N_MEM, D_MODEL), lambda b, i: (layer, b, 0))
    ls = lambda name: _layer_spec(p[name], layer)
    specs = [row_spec(D_MODEL)]
    args = [x]
    if mixer is not None:
        a, g, w_m, m_layer = mixer
        specs += [row_spec(a.shape[1]), row_spec(g.shape[1]), _layer_spec(w_m, m_layer)]
        args += [a, g, w_m]
    specs += [ls("gq"), _resident_spec(w["w_q"]), ls("q_gain"), mem_spec, mem_spec,
              _resident_spec(w["w_xo"]), ls("gf"), _resident_spec(w["w_gu"]),
              _resident_spec(w["w_down"])]
    args += [p["gq"], w["w_q"], p["q_gain"], k_mem, v_mem, w["w_xo"], p["gf"], w["w_gu"],
             w["w_down"]]
    out_specs = [row_spec(D_MODEL)]
    out_shape = [jax.ShapeDtypeStruct(x.shape, F32)]
    n_cast = 0
    if nxt is not None:
        c_in, c_out, c_shapes = _cast_plan([(nxt[name], layer + 1) for name in POST_WEIGHTS],
                                           n_steps, flat)
        specs += c_in
        args += [nxt[name] for name in POST_WEIGHTS]
        out_specs += c_out
        out_shape += c_shapes
        n_cast = len(POST_WEIGHTS)
    outs = pl.pallas_call(
        partial(_post_kernel, has_proj=mixer is not None, n_cast=n_cast),
        grid=(batch, steps),
        in_specs=specs,
        out_specs=out_specs,
        out_shape=out_shape,
        compiler_params=_params(2),
        name="xattn_ffn" if mixer is None else "proj_xattn_ffn",
    )(*args)
    nxt_w = dict(zip(POST_WEIGHTS, outs[1:])) if nxt is not None else None
    return outs[0], nxt_w


def kernel(x, mem, positions, norm_mix, norm_mem_q, norm_mem_kv, norm_ffn, ev_w_in, ev_q_gain, ev_k_gain, ev_w_s, ev_b_s, ev_ln_g, ev_ln_b, ev_w_out, od_w_in, od_w_g1, od_w_g2, od_b_g, od_o_gain, od_w_out, xa_w_q, xa_w_kv, xa_q_gain, xa_k_gain, xa_w_out, ffn_w_gu, ffn_w_down):
    batch, seq, d_model = x.shape
    assert d_model == D_MODEL and seq % ROW_TILE == 0 and seq % MOBA_BLOCK == 0
    assert mem.shape == (batch, N_MEM, D_MODEL)
    xf = x.reshape(batch * seq, D_MODEL)
    memf = mem.reshape(batch * N_MEM, D_MODEL)
    cos_t, sin_t = _rope_tables(positions)
    vec = lambda a: a.reshape(a.shape[0], 1, a.shape[1])
    ev = _ev_params(norm_mix, ev_w_in, ev_q_gain, ev_k_gain, ev_ln_g, ev_ln_b, ev_w_s, ev_b_s)
    ev_w_out_bf = ev_w_out.astype(BF16)
    od = _od_params(norm_mix, od_w_in, od_w_g1, od_w_g2, od_b_g, od_o_gain, od_w_out)
    post = dict(gq=vec(norm_mem_q), q_gain=vec(xa_q_gain), gf=vec(norm_ffn))
    post_f32 = dict(w_q=xa_w_q, w_xo=xa_w_out, w_gu=ffn_w_gu, w_down=ffn_w_down)
    depth = xa_w_kv.shape[0]
    w_kv_rows = xa_w_kv.reshape(depth * D_MODEL, 2 * D_MODEL)
    for l in range(DEPTH):
        i = l // 2
        if l % 2 == 0:
            q_t, k, v_t, g = _ev_proj(xf, ev, i, cos_t, sin_t)
            casts = ([(post_f32[name], 0) for name in POST_WEIGHTS] + [(w_kv_rows, None)]
                     if l == 0 else [])
            a, cast_out = _moba(q_t, k, v_t, batch, seq, casts)
            if l == 0:
                post_w = dict(zip(POST_WEIGHTS, cast_out))
                w_kv_bf = cast_out[-1].reshape(depth, D_MODEL, 2 * D_MODEL)
                k_mem, v_mem = _mem_kv(memf, vec(norm_mem_kv), w_kv_bf, vec(xa_k_gain))
            mixer = (a, g, ev_w_out_bf, i)
        else:
            xf = _od_mixer(xf, od, i, batch, seq)
            mixer = None
        xf, post_w = _xattn_ffn(xf, mixer, post, post_w, post_f32 if l + 1 < DEPTH else None,
                                l, k_mem, v_mem, batch, seq)
    return xf.reshape(batch, seq, D_MODEL)
```

```python
import math
from functools import partial

import jax
import jax.numpy as jnp
from jax import lax
from jax.experimental import pallas as pl
from jax.experimental.pallas import tpu as pltpu

F32 = jnp.float32
BF16 = jnp.bfloat16

D_MODEL = 1024
DEPTH = 4
N_MEM = 256
EPS = 1e-6
NEG_INF = -1e30

HEAD_DIM = 64
ROPE_DIM = HEAD_DIM // 4
ROPE_THETA = 500000.0

A_HEADS = 8
A_WIDTH = A_HEADS * HEAD_DIM
MOBA_BLOCK = 256
MOBA_TOPK = 3
MOBA_LOOKAHEAD = 3

B_GROUPS = 8
B_WIDTH = B_GROUPS * HEAD_DIM
GMLP_CHUNK = 128

GLA_HEADS = 4
GLA_DK = (D_MODEL // 2) // GLA_HEADS
GLA_DV = D_MODEL // GLA_HEADS
GLA_QK_W = GLA_HEADS * GLA_DK
GLA_V_W = GLA_HEADS * GLA_DV
GLA_LOWRANK = 16
GLA_TAU = 16.0
GLA_CHUNK = 64
GLA_BATCH_PAIR = 2
GLA_ATT_BLOCK = 256

X_HEADS = 4
X_HEAD_DIM = D_MODEL // X_HEADS

D_FF = -(-8 * D_MODEL // (3 * 256)) * 256

LANES = 128
VMEM_LIMIT_BYTES = 56 * 1024 * 1024

ROW_TILE = 512
EV_TILES_PER_STEP = 4
FFN_SPLITS = tuple((lo, lo + 256) for lo in range(0, D_FF, 256))


def _params(n_axes):
    return pltpu.CompilerParams(
        dimension_semantics=("arbitrary",) * n_axes,
        vmem_limit_bytes=VMEM_LIMIT_BYTES)


def _layer_spec(stacked, layer):
    nd = stacked.ndim - 1
    return pl.BlockSpec((None,) + stacked.shape[1:], lambda *_: (layer,) + (0,) * nd,
                        pipeline_mode=pl.Buffered(1))


def _dot(a, b):
    return jnp.dot(a, b, preferred_element_type=F32)


def _dot_nt(a, b):
    return lax.dot_general(a, b, (((1,), (1,)), ((), ())),
                           preferred_element_type=F32)


def _dot_tn(a, b):
    return lax.dot_general(a, b, (((0,), (0,)), ((), ())),
                           preferred_element_type=F32)


def _rms_rows(x, gain):
    ms = jnp.mean(x * x, axis=-1, keepdims=True)
    return x * lax.rsqrt(ms + EPS) * gain


def _gelu(x):
    return 0.5 * x * (1.0 + lax.erf(x * math.sqrt(0.5)))


def _silu(x):
    return x / (1.0 + jnp.exp(-x))


def _rope_table_kernel(pos_ref, invf_ref, cos_ref, sin_ref):
    ang = pos_ref[...] * invf_ref[...]
    cos_ref[...] = jnp.cos(ang)
    sin_ref[...] = jnp.sin(ang)


def _rope_tables(positions):
    t = positions.size
    half = ROPE_DIM // 2
    inv_freq = ROPE_THETA ** (-jnp.arange(half, dtype=F32) * 2.0 / ROPE_DIM)
    invf = jnp.broadcast_to(inv_freq[:, None], (half, t))
    pos = positions.astype(F32).reshape(1, t)
    whole = lambda rows: pl.BlockSpec((rows, t), lambda: (0, 0))
    return pl.pallas_call(
        _rope_table_kernel,
        in_specs=[whole(1), whole(half)],
        out_specs=[whole(half), whole(half)],
        out_shape=[jax.ShapeDtypeStruct((half, t), F32)] * 2,
        name="rope_tables",
    )(pos, invf)


def _alternate(chains):
    next(chains[0])
    live = list(chains)
    while live:
        for chain in list(live[::-1]):
            try:
                next(chain)
            except StopIteration:
                live.remove(chain)


def _ev_proj_kernel(x_ref, *refs):
    _alternate([_ev_proj_chain(e, x_ref, *refs) for e in range(x_ref.shape[0] // ROW_TILE)])


def _ev_proj_chain(e, x_ref, gn_ref, wt_ref, qg_ref, kg_ref, cos_ref, sin_ref,
                   lng_ref, lnb_ref, wst_ref, bs_ref,
                   qt_out, k_out, vt_out, g_out):
    tm = ROW_TILE
    tile = slice(e * tm, (e + 1) * tm)
    h = _rms_rows(x_ref[tile, :], gn_ref[...]).astype(BF16)
    cos_t = cos_ref[:, tile]
    sin_t = sin_ref[:, tile]
    half = ROPE_DIM // 2

    def section_t(row0, width):
        return _dot_nt(wt_ref[row0:row0 + width, :], h)

    def normed_rotated_heads(row0, gain_ref):
        sec = section_t(row0, A_WIDTH)
        heads = []
        for hd in range(A_HEADS):
            xh = sec[hd * HEAD_DIM:(hd + 1) * HEAD_DIM]
            ms = jnp.mean(xh * xh, axis=0, keepdims=True)
            y = xh * lax.rsqrt(ms + EPS) * gain_ref[...]
            x1, x2 = y[:half], y[half:ROPE_DIM]
            heads.append(jnp.concatenate(
                [x1 * cos_t - x2 * sin_t, x2 * cos_t + x1 * sin_t, y[ROPE_DIM:]], axis=0))
        return heads

    for hd, y in enumerate(normed_rotated_heads(0, qg_ref)):
        qt_out[hd * HEAD_DIM:(hd + 1) * HEAD_DIM, tile] = y.astype(qt_out.dtype)
    yield
    k_t = jnp.concatenate(normed_rotated_heads(A_WIDTH, kg_ref), axis=0)
    k_out[tile, :] = k_t.T.astype(k_out.dtype)
    yield

    vt_out[:, tile] = section_t(2 * A_WIDTH, A_WIDTH).astype(vt_out.dtype)

    u_t = _gelu(section_t(3 * A_WIDTH, B_WIDTH))
    vg_t = _gelu(section_t(3 * A_WIDTH + B_WIDTH, B_WIDTH))
    yield
    n_chunks = tm // GMLP_CHUNK
    j_i = lax.broadcasted_iota(jnp.int32, (GMLP_CHUNK, GMLP_CHUNK), 0)
    i_i = lax.broadcasted_iota(jnp.int32, (GMLP_CHUNK, GMLP_CHUNK), 1)
    keep = j_i <= i_i
    g_rows = []
    for g in range(B_GROUPS):
        rows = slice(g * HEAD_DIM, (g + 1) * HEAD_DIM)
        xg = vg_t[rows]
        xg = xg - jnp.mean(xg, axis=0, keepdims=True)
        var = jnp.mean(xg * xg, axis=0, keepdims=True)
        vn = (xg * lax.rsqrt(var + EPS) * lng_ref[rows] + lnb_ref[rows]).astype(BF16)
        stacked = jnp.concatenate(
            [vn[:, r * GMLP_CHUNK:(r + 1) * GMLP_CHUNK] for r in range(n_chunks)], axis=0)
        w_t = jnp.where(keep, wst_ref[g], 0.0).astype(BF16)
        mixed = _dot(stacked, w_t) + bs_ref[g:g + 1, :]
        mixed = jnp.concatenate(
            [mixed[r * HEAD_DIM:(r + 1) * HEAD_DIM] for r in range(n_chunks)], axis=1)
        g_rows.append(u_t[rows] * mixed)
        if g == B_GROUPS // 2 - 1:
            yield
    g_out[tile, :] = jnp.concatenate(g_rows, axis=0).T.astype(g_out.dtype)


def _ev_params(norm_mix, w_in, q_gain, k_gain, ln_g, ln_b, w_s, b_s):
    n_ev = w_in.shape[0]
    tm = ROW_TILE
    along_lanes = lambda a: jnp.broadcast_to(a[:, :, None], a.shape + (tm,))
    gn = norm_mix[0::2].reshape(n_ev, 1, D_MODEL)
    w_t = jnp.swapaxes(w_in.astype(BF16), 1, 2)
    qg = along_lanes(q_gain)
    kg = along_lanes(k_gain)
    lng = along_lanes(ln_g.reshape(n_ev, B_WIDTH))
    lnb = along_lanes(ln_b.reshape(n_ev, B_WIDTH))
    w_s_t = jnp.swapaxes(w_s, 2, 3)
    return (gn, w_t, qg, kg), (lng, lnb, w_s_t, b_s)


def _ev_proj(x, params, layer, cos_t, sin_t):
    t = x.shape[0]
    tm = ROW_TILE * EV_TILES_PER_STEP
    row_spec = lambda n: pl.BlockSpec((tm, n), lambda i: (i, 0))
    col_spec = pl.BlockSpec((A_WIDTH, tm), lambda i: (0, i))
    rope_spec = pl.BlockSpec((ROPE_DIM // 2, tm), lambda i: (0, i))
    head, tail = params
    return pl.pallas_call(
        _ev_proj_kernel,
        grid=(t // tm,),
        in_specs=([row_spec(D_MODEL)] + [_layer_spec(a, layer) for a in head]
                  + [rope_spec, rope_spec] + [_layer_spec(a, layer) for a in tail]),
        out_specs=[col_spec, row_spec(A_WIDTH), col_spec, row_spec(B_WIDTH)],
        out_shape=[jax.ShapeDtypeStruct((A_WIDTH, t), BF16),
                   jax.ShapeDtypeStruct((t, A_WIDTH), BF16),
                   jax.ShapeDtypeStruct((A_WIDTH, t), BF16),
                   jax.ShapeDtypeStruct((t, B_WIDTH), BF16)],
        compiler_params=_params(1),
        name="ev_proj",
    )(x, *head, cos_t, sin_t, *tail)


def _moba_kernel(qt_ref, k_ref, vt_ref, o_ref, s_ref):
    seq = k_ref.shape[0]
    nb = seq // MOBA_BLOCK
    blk = MOBA_BLOCK
    scale = HEAD_DIM ** -0.5

    k_mean = jnp.mean(k_ref[...].astype(F32).reshape(nb, blk, LANES), axis=1)
    k_mean = jnp.concatenate([k_mean, jnp.zeros((16 - nb, LANES), F32)], axis=0).astype(BF16)

    lo = lax.broadcasted_iota(jnp.int32, (LANES, blk), 0) < HEAD_DIM
    key_i = lax.broadcasted_iota(jnp.int32, (blk, blk), 0)
    qry_i = lax.broadcasted_iota(jnp.int32, (blk, blk), 1)
    causal = key_i <= qry_i

    def scores(i, hd, slot):
        q_i = qt_ref[:, i * blk:(i + 1) * blk]
        head_mask = lo if hd == 0 else jnp.logical_not(lo)
        qh = jnp.where(head_mask, q_i, jnp.zeros_like(q_i)) * scale
        col_max = []
        for j in range(i + 1):
            sj = _dot(k_ref[j * blk:(j + 1) * blk, :], qh)
            if j == i:
                sj = jnp.where(causal, sj, NEG_INF)
            s_ref[slot, j * blk:(j + 1) * blk, :] = sj
            col_max.append(jnp.max(sj, axis=0, keepdims=True))
            yield
        if i > MOBA_TOPK:
            gate = _dot(k_mean, qh)
            g = [gate[j:j + 1, :] for j in range(i)]
            bias = []
            for j in range(i):
                cnt = jnp.zeros_like(g[j])
                for jp in range(i):
                    if jp == j:
                        continue
                    ahead = (g[jp] >= g[j]) if jp < j else (g[jp] > g[j])
                    cnt = cnt + jnp.where(ahead, 1.0, 0.0)
                bias.append(jnp.where(cnt < MOBA_TOPK, 0.0, NEG_INF))
        else:
            bias = [None] * i
        m = col_max[i]
        for j in range(i):
            m = jnp.maximum(m, col_max[j] if bias[j] is None else col_max[j] + bias[j])
        shifts = [-m if bias[j] is None else bias[j] - m for j in range(i)] + [-m]
        return slot, shifts

    def weighted_values(i, hd, slot, shifts):
        ones = jnp.ones((16, blk), BF16)
        acc = jnp.zeros((HEAD_DIM + 16, blk), F32)
        for j in range(i + 1):
            sj = s_ref[slot, j * blk:(j + 1) * blk, :]
            p = jnp.exp(sj + shifts[j]).astype(BF16)
            v_aug = jnp.concatenate(
                [vt_ref[hd * HEAD_DIM:(hd + 1) * HEAD_DIM, j * blk:(j + 1) * blk], ones], axis=0)
            acc = acc + _dot(v_aug, p)
            yield
        return acc[:HEAD_DIM] / acc[HEAD_DIM:HEAD_DIM + 1]

    def interleave(*stages):
        results = [None] * len(stages)
        live = list(enumerate(stages))
        while live:
            for idx, stage in list(live):
                try:
                    next(stage)
                except StopIteration as done:
                    results[idx] = done.value
                    live.remove((idx, stage))
        return results

    groups = [(i, hd) for i in range(nb) for hd in range(LANES // HEAD_DIM)]
    n_slots = s_ref.shape[0]
    pending = interleave(*[scores(*g, n % n_slots)
                           for n, g in enumerate(groups[:MOBA_LOOKAHEAD])])
    head_out = []
    for n, (i, hd) in enumerate(groups):
        stages = [weighted_values(i, hd, *pending.pop(0))]
        if n + MOBA_LOOKAHEAD < len(groups):
            ahead = n + MOBA_LOOKAHEAD
            stages.append(scores(*groups[ahead], ahead % n_slots))
        done = interleave(*stages)
        head_out.append(done[0])
        if len(done) > 1:
            pending.append(done[1])
        if hd == LANES // HEAD_DIM - 1:
            o_t = jnp.concatenate(head_out, axis=0)
            o_ref[i * blk:(i + 1) * blk, :] = o_t.T.astype(o_ref.dtype)
            head_out = []


def _cast_plan(sources, n_steps, flat_step):
    in_specs, out_specs, out_shapes = [], [], []
    for arr, layer in sources:
        rows, cols = arr.shape[-2:]
        visits = 1 if rows % (16 * n_steps) == 0 else 2
        band = rows * visits // n_steps
        assert band % 16 == 0 and band * n_steps == rows * visits
        if layer is None:
            in_specs.append(pl.BlockSpec(
                (band, cols), lambda *g, v=visits: (flat_step(*g) // v, 0)))
        else:
            in_specs.append(pl.BlockSpec(
                (None, band, cols), lambda *g, v=visits, l=layer: (l, flat_step(*g) // v, 0)))
        out_specs.append(pl.BlockSpec((band, cols), lambda *g, v=visits: (flat_step(*g) // v, 0)))
        out_shapes.append(jax.ShapeDtypeStruct((rows, cols), BF16))
    return in_specs, out_specs, out_shapes


def _moba_cast_kernel(qt_ref, k_ref, vt_ref, *refs, n_cast):
    cast_src, o_ref, cast_dst, s_ref = (refs[:n_cast], refs[n_cast],
                                        refs[n_cast + 1:2 * n_cast + 1], refs[-1])
    for src, dst in zip(cast_src, cast_dst):
        dst[...] = src[...].astype(dst.dtype)
    _moba_kernel(qt_ref, k_ref, vt_ref, o_ref, s_ref)


def _moba(q_t, k, v_t, batch, seq, cast_sources=()):
    n_pairs = A_WIDTH // LANES
    row_major = pl.BlockSpec((seq, LANES), lambda b, p: (b, p))
    feat_major = pl.BlockSpec((LANES, seq), lambda b, p: (p, b))
    c_in, c_out, c_shapes = _cast_plan(cast_sources, batch * n_pairs,
                                       lambda b, p: b * n_pairs + p)
    outs = pl.pallas_call(
        partial(_moba_cast_kernel, n_cast=len(cast_sources)),
        grid=(batch, n_pairs),
        in_specs=[feat_major, row_major, feat_major] + c_in,
        out_specs=[row_major] + c_out,
        out_shape=[jax.ShapeDtypeStruct(k.shape, BF16)] + c_shapes,
        scratch_shapes=[pltpu.VMEM((MOBA_LOOKAHEAD + 1, seq, MOBA_BLOCK), F32)],
        compiler_params=_params(2),
        name="moba_attn",
    )(q_t, k, v_t, *[arr for arr, _ in cast_sources])
    return outs[0], outs[1:]


def _od_mixer_kernel(x_ref, gn_ref, w_ref, wg1_ref, wg2_ref, bg_ref, og_ref, wo_ref,
                     o_ref, state_ref):
    @pl.when(pl.program_id(1) == 0)
    def _():
        state_ref[...] = jnp.zeros_like(state_ref)

    _alternate([_od_mixer_chain(e, x_ref, gn_ref, w_ref, wg1_ref, wg2_ref, bg_ref, og_ref,
                                wo_ref, o_ref, state_ref) for e in range(x_ref.shape[0])])


def _od_mixer_chain(e, x_ref, gn_ref, w_ref, wg1_ref, wg2_ref, bg_ref, og_ref, wo_ref,
                    o_ref, state_ref):
    tm = x_ref.shape[1]
    c = GLA_CHUNK
    nc = tm // c
    x = x_ref[e]
    h = _rms_rows(x, gn_ref[...]).astype(BF16)
    q = _dot(h, w_ref[:, :GLA_QK_W]) * (GLA_DK ** -0.5)
    k = _dot(h, w_ref[:, GLA_QK_W:2 * GLA_QK_W])
    z = _dot(_dot(h, wg1_ref[...]).astype(BF16), wg2_ref[...]) + bg_ref[...]
    log_a = (jnp.minimum(z, 0.0) - jnp.log(1.0 + jnp.exp(-jnp.abs(z)))) * (1.0 / GLA_TAU)

    in_chunk = lax.broadcasted_iota(jnp.int32, (tm, GLA_QK_W), 0) % c
    b = log_a
    step = 1
    while step < c:
        b = b + jnp.where(in_chunk >= step, pltpu.roll(b, step, 0), 0.0)
        step *= 2
    ab = GLA_ATT_BLOCK
    t_i = lax.broadcasted_iota(jnp.int32, (ab, ab), 0)
    s_i = lax.broadcasted_iota(jnp.int32, (ab, ab), 1)
    causal = jnp.logical_and(t_i // c == s_i // c, s_i <= t_i)
    b3 = b.reshape(nc, c, GLA_QK_W)
    b_last = b3[:, c - 1:c, :]
    q_in = (q * jnp.exp(b)).astype(BF16)
    k_in = (k * jnp.exp(-b)).astype(BF16)
    k_state = (k.reshape(nc, c, GLA_QK_W) * jnp.exp(b_last - b3)).reshape(tm, GLA_QK_W)
    k_state = k_state.astype(BF16)
    decay = jnp.exp(b_last)
    yield

    def intra(hd):
        ks = slice(hd * GLA_DK, (hd + 1) * GLA_DK)
        v0 = 2 * GLA_QK_W + hd * GLA_DV
        r0 = 2 * GLA_QK_W + GLA_V_W + hd * GLA_DV
        v_h = _dot(h, w_ref[:, v0:v0 + GLA_DV]).astype(BF16)
        r_h = _dot(h, w_ref[:, r0:r0 + GLA_DV])
        o_intra = []
        for a0 in range(0, tm, ab):
            rows = slice(a0, a0 + ab)
            att = jnp.where(causal, _dot_nt(q_in[rows, ks], k_in[rows, ks]), 0.0)
            o_intra.append(_dot(att.astype(BF16), v_h[rows]))
        return v_h, r_h, jnp.concatenate(o_intra, axis=0)

    def recurrent(hd, v_h, r_h, o_intra):
        ks = slice(hd * GLA_DK, (hd + 1) * GLA_DK)
        state = state_ref[e, hd]
        parts = []
        for n in range(nc):
            rs = slice(n * c, (n + 1) * c)
            parts.append(o_intra[rs] + _dot_nt(q_in[rs, ks], state.astype(BF16)))
            state = decay[n][:, ks] * state + _dot_tn(v_h[rs], k_state[rs, ks])
        state_ref[e, hd] = state
        o = jnp.concatenate(parts, axis=0)
        o = _rms_rows(o, og_ref[:, hd * GLA_DV:(hd + 1) * GLA_DV])
        return (o * _silu(r_h)).astype(BF16)

    heads = []
    for hd in range(GLA_HEADS):
        pre = intra(hd)
        yield
        heads.append(recurrent(hd, *pre))
        yield
    o_ref[e] = x + _dot(jnp.concatenate(heads, axis=-1), wo_ref[...])


def _od_params(norm_mix, w_g1, w_g2, b_g, o_gain):
    n_od = w_g1.shape[0]
    wg1 = jnp.pad(w_g1, ((0, 0), (0, 0), (0, LANES - GLA_LOWRANK))).astype(BF16)
    wg2 = jnp.pad(w_g2, ((0, 0), (0, LANES - GLA_LOWRANK), (0, 0))).astype(BF16)
    return (norm_mix[1::2].reshape(n_od, 1, D_MODEL), wg1, wg2,
            b_g.reshape(n_od, 1, GLA_QK_W), o_gain.reshape(n_od, 1, GLA_V_W))


def _od_mixer(x, params, w_in, w_out, layer, batch, seq):
    tm = ROW_TILE
    pair = GLA_BATCH_PAIR
    assert batch % pair == 0
    x3 = x.reshape(batch, seq, D_MODEL)
    tile_spec = pl.BlockSpec((pair, tm, D_MODEL), lambda b, i: (b, i, 0))
    gn, wg1, wg2, bg, og = params
    stacked = lambda a: _layer_spec(a, layer)
    out = pl.pallas_call(
        _od_mixer_kernel,
        grid=(batch // pair, seq // tm),
        in_specs=[tile_spec, stacked(gn), _resident_spec(w_in), stacked(wg1), stacked(wg2),
                  stacked(bg), stacked(og), _resident_spec(w_out)],
        out_specs=tile_spec,
        out_shape=jax.ShapeDtypeStruct(x3.shape, F32),
        scratch_shapes=[pltpu.VMEM((pair, GLA_HEADS, GLA_DV, GLA_DK), F32)],
        compiler_params=_params(2),
        name="gla_mixer",
    )(x3, gn, w_in, wg1, wg2, bg, og, w_out)
    return out.reshape(x.shape)


def _mem_kv_kernel(m_ref, gn_ref, w_ref, kg_ref, k_out, v_out):
    h = _rms_rows(m_ref[...], gn_ref[...]).astype(BF16)
    k = _dot(h, w_ref[:, :D_MODEL])
    for hh in range(X_HEADS):
        sl = slice(hh * X_HEAD_DIM, (hh + 1) * X_HEAD_DIM)
        k_out[:, sl] = _rms_rows(k[:, sl], kg_ref[...]).astype(k_out.dtype)
    v_out[...] = _dot(h, w_ref[:, D_MODEL:]).astype(v_out.dtype)


def _mem_kv(mem, gn, w_kv, k_gain):
    t = mem.shape[0]
    depth = w_kv.shape[0]
    tm = ROW_TILE
    per_layer = lambda a: pl.BlockSpec((None,) + a.shape[1:], lambda l, i: (l, 0, 0))
    out_spec = pl.BlockSpec((None, tm, D_MODEL), lambda l, i: (l, i, 0))
    return pl.pallas_call(
        _mem_kv_kernel,
        grid=(depth, t // tm),
        in_specs=[pl.BlockSpec((tm, D_MODEL), lambda l, i: (i, 0)),
                  per_layer(gn), per_layer(w_kv), per_layer(k_gain)],
        out_specs=[out_spec, out_spec],
        out_shape=[jax.ShapeDtypeStruct((depth, t, D_MODEL), BF16)] * 2,
        compiler_params=_params(2),
        name="mem_kv",
    )(mem, gn, w_kv, k_gain)


def _xattn_ffn_body(x, gq_ref, wq_ref, qg_ref, k_ref, v_ref, wo_ref,
                    gf_ref, wgu_ref, wd_ref, o_ref):
    h = _rms_rows(x, gq_ref[...]).astype(BF16)
    q = _dot(h, wq_ref[...])
    scale = X_HEAD_DIM ** -0.5
    heads = []
    for hh in range(X_HEADS):
        sl = slice(hh * X_HEAD_DIM, (hh + 1) * X_HEAD_DIM)
        qh = _rms_rows(q[:, sl], qg_ref[...]).astype(BF16)
        s = _dot_nt(qh, k_ref[:, sl]) * scale
        e = jnp.exp(s - jnp.max(s, axis=-1, keepdims=True))
        p = (e / jnp.sum(e, axis=-1, keepdims=True)).astype(BF16)
        heads.append(_dot(p, v_ref[:, sl]).astype(BF16))
    x = x + _dot(jnp.concatenate(heads, axis=-1), wo_ref[...])

    h = _rms_rows(x, gf_ref[...]).astype(BF16)
    acc = x
    for lo, hi in FFN_SPLITS:
        g = _dot(h, wgu_ref[:, lo:hi])
        u = _dot(h, wgu_ref[:, D_FF + lo:D_FF + hi])
        acc = acc + _dot((_silu(g) * u).astype(BF16), wd_ref[lo:hi, :])
    o_ref[...] = acc


POST_WEIGHTS = ("w_q", "w_xo", "w_gu", "w_down")
N_POST_OPERANDS = 9


def _post_kernel(*refs, has_proj, n_cast):
    refs = list(refs)
    x = refs.pop(0)[...]
    if has_proj:
        a_ref, g_ref, wm_ref = refs[:3]
        del refs[:3]
        n_a = a_ref.shape[1]
        x = x + _dot(a_ref[...], wm_ref[:n_a, :]) + _dot(g_ref[...], wm_ref[n_a:, :])
    operands = refs[:N_POST_OPERANDS]
    cast_src = refs[N_POST_OPERANDS:N_POST_OPERANDS + n_cast]
    o_ref = refs[N_POST_OPERANDS + n_cast]
    cast_dst = refs[N_POST_OPERANDS + n_cast + 1:]
    for src, dst in zip(cast_src, cast_dst):
        dst[...] = src[...].astype(dst.dtype)
    _xattn_ffn_body(x, *operands, o_ref)


def _resident_spec(arr):
    nd = arr.ndim
    return pl.BlockSpec(arr.shape, lambda *_: (0,) * nd, pipeline_mode=pl.Buffered(1))


def _xattn_ffn(x, mixer, p, w, cast_sources, layer, k_mem, v_mem, batch, seq):
    tm = ROW_TILE
    steps = seq // tm
    n_steps = batch * steps
    flat = lambda b, i: b * steps + i
    row_spec = lambda n: pl.BlockSpec((tm, n), lambda b, i: (flat(b, i), 0))
    mem_spec = pl.BlockSpec((None, N_MEM, D_MODEL), lambda b, i: (layer, b, 0))
    ls = lambda name: _layer_spec(p[name], layer)
    specs = [row_spec(D_MODEL)]
    args = [x]
    if mixer is not None:
        a, g, w_m = mixer
        specs += [row_spec(a.shape[1]), row_spec(g.shape[1]), _resident_spec(w_m)]
        args += [a, g, w_m]
    specs += [ls("gq"), _resident_spec(w["w_q"]), ls("q_gain"), mem_spec, mem_spec,
              _resident_spec(w["w_xo"]), ls("gf"), _resident_spec(w["w_gu"]),
              _resident_spec(w["w_down"])]
    args += [p["gq"], w["w_q"], p["q_gain"], k_mem, v_mem, w["w_xo"], p["gf"], w["w_gu"],
             w["w_down"]]
    c_in, c_out, c_shapes = _cast_plan(cast_sources, n_steps, flat)
    outs = pl.pallas_call(
        partial(_post_kernel, has_proj=mixer is not None, n_cast=len(cast_sources)),
        grid=(batch, steps),
        in_specs=specs + c_in,
        out_specs=[row_spec(D_MODEL)] + c_out,
        out_shape=[jax.ShapeDtypeStruct(x.shape, F32)] + c_shapes,
        compiler_params=_params(2),
        name="xattn_ffn" if mixer is None else "proj_xattn_ffn",
    )(*args, *[arr for arr, _ in cast_sources])
    return outs[0], outs[1:]


def kernel(x, mem, positions, norm_mix, norm_mem_q, norm_mem_kv, norm_ffn, ev_w_in, ev_q_gain, ev_k_gain, ev_w_s, ev_b_s, ev_ln_g, ev_ln_b, ev_w_out, od_w_in, od_w_g1, od_w_g2, od_b_g, od_o_gain, od_w_out, xa_w_q, xa_w_kv, xa_q_gain, xa_k_gain, xa_w_out, ffn_w_gu, ffn_w_down):
    batch, seq, d_model = x.shape
    assert d_model == D_MODEL and seq % ROW_TILE == 0 and seq % MOBA_BLOCK == 0
    assert mem.shape == (batch, N_MEM, D_MODEL)
    xf = x.reshape(batch * seq, D_MODEL)
    memf = mem.reshape(batch * N_MEM, D_MODEL)
    cos_t, sin_t = _rope_tables(positions)
    vec = lambda a: a.reshape(a.shape[0], 1, a.shape[1])
    ev = _ev_params(norm_mix, ev_w_in, ev_q_gain, ev_k_gain, ev_ln_g, ev_ln_b, ev_w_s, ev_b_s)
    od = _od_params(norm_mix, od_w_g1, od_w_g2, od_b_g, od_o_gain)
    post = dict(gq=vec(norm_mem_q), q_gain=vec(xa_q_gain), gf=vec(norm_ffn))
    post_f32 = dict(w_q=xa_w_q, w_xo=xa_w_out, w_gu=ffn_w_gu, w_down=ffn_w_down)
    depth = xa_w_kv.shape[0]
    w_kv_rows = xa_w_kv.reshape(depth * D_MODEL, 2 * D_MODEL)

    def layer_weights(l):
        own = [(ev_w_out, l // 2)] if l % 2 == 0 else [(od_w_in, l // 2), (od_w_out, l // 2)]
        return own + [(post_f32[name], l) for name in POST_WEIGHTS]

    n_post = len(POST_WEIGHTS)
    for l in range(DEPTH):
        i = l // 2
        if l % 2 == 0:
            q_t, k, v_t, g = _ev_proj(xf, ev, i, cos_t, sin_t)
            casts = layer_weights(0) + [(w_kv_rows, None)] if l == 0 else []
            a, cast_out = _moba(q_t, k, v_t, batch, seq, casts)
            if l == 0:
                w_kv_bf = cast_out[-1].reshape(depth, D_MODEL, 2 * D_MODEL)
                k_mem, v_mem = _mem_kv(memf, vec(norm_mem_kv), w_kv_bf, vec(xa_k_gain))
                ready = list(cast_out[:-1])
            mixer = (a, g, ready[0])
        else:
            xf = _od_mixer(xf, od, ready[0], ready[1], i, batch, seq)
            mixer = None
        post_w = dict(zip(POST_WEIGHTS, ready[-n_post:]))
        xf, ready = _xattn_ffn(xf, mixer, post, post_w,
                               layer_weights(l + 1) if l + 1 < DEPTH else [],
                               l, k_mem, v_mem, batch, seq)
    return xf.reshape(batch, seq, D_MODEL)
```
